```python
import math
import jax, jax.numpy as jnp
from jax import lax
import numpy as np

D_MODEL = 1024
BATCH = 1
SEQ = 16384
DEPTH = 1
DEC_BATCH = 16
DEC_SEQ = 16
PAST_LEN = 4096

CHUNK = 64
N_META = 16
D_MIX = D_MODEL
D_LRU = D_MIX // 2
LRU_BLOCKS = 8
LRU_BLK = D_LRU // LRU_BLOCKS
CONV_W = 4
LRU_C = 8.0
N_HEADS = 4
HEAD_DIM = 64
V_DIM = 2 * HEAD_DIM
D_QK = N_HEADS * 2 * HEAD_DIM
D_ATT = N_HEADS * V_DIM
D_IN = 2 * D_LRU + 2 * D_QK + 2 * D_ATT
SPLITS = (D_LRU, 2 * D_LRU, 2 * D_LRU + D_QK, 2 * D_LRU + 2 * D_QK, 2 * D_LRU + 2 * D_QK + D_ATT)
Q_BLOCK = 128
ROPE_THETA = 10000.0
EPS = 1e-6
SCALE = HEAD_DIM ** -0.5
NEG_INF = -1e30

kernel_name = 'hybrid_rglru_diffattn_stream'


def lambda_init(layer_idx):
    return 0.8 - 0.6 * math.exp(-0.3 * layer_idx)


def rmsnorm(x, w):
    xf = x.astype(jnp.float32)
    y = xf * lax.rsqrt(jnp.mean(xf * xf, axis=-1, keepdims=True) + EPS)
    return (y * w.astype(jnp.float32)).astype(x.dtype)


def rope(x, pos):
    half = HEAD_DIM // 2
    inv = ROPE_THETA ** (-jnp.arange(0, HEAD_DIM, 2, dtype=jnp.float32) / HEAD_DIM)
    ang = pos.astype(jnp.float32)[:, None] * inv[None, :]
    cos = jnp.cos(ang)[None, :, None, None, :]
    sin = jnp.sin(ang)[None, :, None, None, :]
    xf = x.astype(jnp.float32)
    x1, x2 = xf[..., :half], xf[..., half:]
    return jnp.concatenate([x1 * cos - x2 * sin, x1 * sin + x2 * cos], axis=-1).astype(x.dtype)


def rglru_branch(xa, conv_prev, h0, conv_w, conv_b, ga_w, ga_b, gx_w, gx_b, lru_lam):
    B, T, _ = xa.shape
    xp = jnp.concatenate([conv_prev.astype(xa.dtype), xa], axis=1)
    conv = conv_b + sum(xp[:, j:j + T] * conv_w[j] for j in range(CONV_W))
    cf = conv.astype(jnp.float32)
    cb = cf.reshape(B, T, LRU_BLOCKS, LRU_BLK)
    r = jax.nn.sigmoid(jnp.einsum('btnc,ncd->btnd', cb, ga_w.astype(jnp.float32)) + ga_b.astype(jnp.float32)).reshape(B, T, D_LRU)
    i = jax.nn.sigmoid(jnp.einsum('btnc,ncd->btnd', cb, gx_w.astype(jnp.float32)) + gx_b.astype(jnp.float32)).reshape(B, T, D_LRU)
    log_a = -LRU_C * r * jax.nn.softplus(-lru_lam.astype(jnp.float32))
    a = jnp.exp(log_a)
    b = jnp.sqrt(-jnp.expm1(2.0 * log_a)) * (i * cf)
    b = b.at[:, 0].add(a[:, 0] * h0.astype(jnp.float32))
    def combine(u, v):
        return (u[0] * v[0], v[0] * u[1] + v[1])
    h = lax.associative_scan(combine, (a, b), axis=1)[1]
    return h.astype(xa.dtype), h[:, -1].astype(h0.dtype), xp[:, -(CONV_W - 1):]


def diff_core(q, k, v, mask, lam):
    s = jnp.einsum('bqhcd,bkhcd->bhcqk', q.astype(jnp.float32), k.astype(jnp.float32)) * SCALE
    if mask is not None:
        s = jnp.where(mask, s, NEG_INF)
    p = jax.nn.softmax(s, axis=-1)
    pd = p[:, :, 0] - lam * p[:, :, 1]
    o = jnp.einsum('bhqk,bkhe->bqhe', pd, v.astype(jnp.float32))
    return o.astype(v.dtype)


def attend_prompt(q, k, v, lam):
    B, T = q.shape[0], q.shape[1]
    S = T - N_META
    n_blk = S // Q_BLOCK
    chunk_k = jnp.concatenate([jnp.full((N_META,), -1, jnp.int32), jnp.arange(S, dtype=jnp.int32) // CHUNK])
    o_meta = diff_core(q[:, :N_META], k[:, :N_META], v[:, :N_META], None, lam)
    qb = q[:, N_META:].reshape(B, n_blk, Q_BLOCK, N_HEADS, 2, HEAD_DIM).transpose(1, 0, 2, 3, 4, 5)
    def block(args):
        qi, bi = args
        chunk_q = (bi * Q_BLOCK + jnp.arange(Q_BLOCK, dtype=jnp.int32)) // CHUNK
        mask = chunk_k[None, :] <= chunk_q[:, None]
        return diff_core(qi, k, v, mask, lam)
    ob = lax.map(block, (qb, jnp.arange(n_blk, dtype=jnp.int32)))
    o_real = ob.transpose(1, 0, 2, 3, 4).reshape(B, S, N_HEADS, V_DIM)
    return jnp.concatenate([o_meta, o_real], axis=1)


def layer_forward(x, pos, h0, conv_prev, kv_past, lw, layer_idx):
    (norm_w, w_in, w_out, conv_w, conv_b, ga_w, ga_b, gx_w, gx_b, lru_lam,
     qn_w, kn_w, lq1, lk1, lq2, lk2, subln_w) = lw
    B, T, _ = x.shape
    u = rmsnorm(x, norm_w) @ w_in
    xa, ga, q, k, v, gb = jnp.split(u, SPLITS, axis=-1)
    h_seq, h_last, conv_tail = rglru_branch(xa, conv_prev, h0, conv_w, conv_b, ga_w, ga_b, gx_w, gx_b, lru_lam)
    y_a = h_seq * jax.nn.silu(ga)
    q = rope(rmsnorm(q.reshape(B, T, N_HEADS, 2, HEAD_DIM), qn_w), pos)
    k = rope(rmsnorm(k.reshape(B, T, N_HEADS, 2, HEAD_DIM), kn_w), pos)
    v = v.reshape(B, T, N_HEADS, V_DIM)
    lam0 = lambda_init(layer_idx)
    lam = (jnp.exp(jnp.sum(lq1.astype(jnp.float32) * lk1.astype(jnp.float32)))
           - jnp.exp(jnp.sum(lq2.astype(jnp.float32) * lk2.astype(jnp.float32))) + lam0)
    if kv_past is None:
        o = attend_prompt(q, k, v, lam)
    else:
        k_all = jnp.concatenate([kv_past[0].astype(k.dtype), k], axis=1)
        v_all = jnp.concatenate([kv_past[1].astype(v.dtype), v], axis=1)
        o = diff_core(q, k_all, v_all, None, lam)
    o = rmsnorm(o, subln_w) * (1.0 - lam0)
    y_b = o.reshape(B, T, D_ATT) * jax.nn.silu(gb)
    y = x + jnp.concatenate([y_a, y_b], axis=-1) @ w_out
    return y, k, v, h_last, conv_tail


def setup_inputs(seed: int = 0) -> dict:
    key = jax.random.key(seed)
    ks = jax.random.split(key, 24)
    f32 = jnp.float32
    nrm = lambda kk, shape, s: jax.random.normal(kk, shape, f32) * s
    u = jax.random.uniform(ks[16], (DEPTH, D_LRU), f32, minval=0.9, maxval=0.999)
    return {
        'x_prompt': nrm(ks[0], (BATCH, SEQ, D_MODEL), 1.0),
        'x_sample': nrm(ks[1], (DEC_BATCH, DEC_SEQ, D_MODEL), 1.0),
        'cache_k': nrm(ks[2], (DEPTH, DEC_BATCH, PAST_LEN, N_HEADS, 2, HEAD_DIM), 1.0),
        'cache_v': nrm(ks[3], (DEPTH, DEC_BATCH, PAST_LEN, N_HEADS, V_DIM), 1.0),
        'state_h': nrm(ks[4], (DEPTH, DEC_BATCH, D_LRU), 0.5),
        'state_conv': nrm(ks[5], (DEPTH, DEC_BATCH, CONV_W - 1, D_LRU), 1.0),
        'meta_tokens': nrm(ks[6], (N_META, D_MODEL), 1.0),
        'norm_w': 1.0 + nrm(ks[7], (DEPTH, D_MODEL), 0.02),
        'w_in': nrm(ks[8], (DEPTH, D_MODEL, D_IN), D_MODEL ** -0.5),
        'w_out': nrm(ks[9], (DEPTH, D_MIX, D_MODEL), D_MIX ** -0.5),
        'conv_w': nrm(ks[10], (DEPTH, CONV_W, D_LRU), CONV_W ** -0.5),
        'conv_b': nrm(ks[11], (DEPTH, D_LRU), 0.01),
        'gate_a_w': nrm(ks[12], (DEPTH, LRU_BLOCKS, LRU_BLK, LRU_BLK), LRU_BLK ** -0.5),
        'gate_a_b': nrm(ks[13], (DEPTH, LRU_BLOCKS, LRU_BLK), 0.01),
        'gate_x_w': nrm(ks[14], (DEPTH, LRU_BLOCKS, LRU_BLK, LRU_BLK), LRU_BLK ** -0.5),
        'gate_x_b': nrm(ks[15], (DEPTH, LRU_BLOCKS, LRU_BLK), 0.01),
        'lru_lambda': jnp.log(u) - jnp.log1p(-u),
        'q_norm_w': 1.0 + nrm(ks[17], (DEPTH, HEAD_DIM), 0.02),
        'k_norm_w': 1.0 + nrm(ks[18], (DEPTH, HEAD_DIM), 0.02),
        'lambda_q1': nrm(ks[19], (DEPTH, HEAD_DIM), 0.1),
        'lambda_k1': nrm(ks[20], (DEPTH, HEAD_DIM), 0.1),
        'lambda_q2': nrm(ks[21], (DEPTH, HEAD_DIM), 0.1),
        'lambda_k2': nrm(ks[22], (DEPTH, HEAD_DIM), 0.1),
        'subln_w': 1.0 + nrm(ks[23], (DEPTH, V_DIM), 0.02),
    }


def reference(x_prompt, x_sample, cache_k, cache_v, state_h, state_conv, meta_tokens,
              norm_w, w_in, w_out, conv_w, conv_b, gate_a_w, gate_a_b, gate_x_w, gate_x_b,
              lru_lambda, q_norm_w, k_norm_w, lambda_q1, lambda_k1, lambda_q2, lambda_k2, subln_w):
    B, S, _ = x_prompt.shape
    past = cache_k.shape[2]
    s_new = x_sample.shape[1]
    xp = jnp.concatenate([jnp.broadcast_to(meta_tokens[None].astype(x_prompt.dtype), (B, N_META, D_MODEL)), x_prompt], axis=1)
    xs = x_sample
    pos_p = jnp.arange(N_META + S, dtype=jnp.int32)
    pos_s = past + jnp.arange(s_new, dtype=jnp.int32)
    h_zero = jnp.zeros((B, D_LRU), jnp.float32)
    conv_zero = jnp.zeros((B, CONV_W - 1, D_LRU), x_prompt.dtype)
    kp_l, vp_l, hp_l, cp_l, ks_l, vs_l, hs_l, cs_l = [], [], [], [], [], [], [], []
    for l in range(DEPTH):
        lw = (norm_w[l], w_in[l], w_out[l], conv_w[l], conv_b[l], gate_a_w[l], gate_a_b[l],
              gate_x_w[l], gate_x_b[l], lru_lambda[l], q_norm_w[l], k_norm_w[l],
              lambda_q1[l], lambda_k1[l], lambda_q2[l], lambda_k2[l], subln_w[l])
        xp, kp, vp, hp, cp = layer_forward(xp, pos_p, h_zero, conv_zero, None, lw, l)
        xs, kss, vss, hss, css = layer_forward(xs, pos_s, state_h[l], state_conv[l], (cache_k[l], cache_v[l]), lw, l)
        kp_l.append(kp); vp_l.append(vp); hp_l.append(hp); cp_l.append(cp)
        ks_l.append(kss); vs_l.append(vss); hs_l.append(hss); cs_l.append(css)
    y_prompt = xp[:, N_META:]
    y_sample = xs
    return (y_prompt, y_sample,
            jnp.stack(kp_l), jnp.stack(vp_l), jnp.stack(hp_l), jnp.stack(cp_l),
            jnp.stack(ks_l), jnp.stack(vs_l), jnp.stack(hs_l), jnp.stack(cs_l))
```

```python
import functools
import math

import jax
import jax.numpy as jnp
from jax import lax
from jax.experimental import pallas as pl
from jax.experimental.pallas import tpu as pltpu

F32 = jnp.float32
BF16 = jnp.bfloat16

D_MODEL = 1024
N_META = 16
CHUNK = 64
CHUNK_SHIFT = 6
HEAD_SHIFT = 6
D_LRU = 512
LRU_BLOCKS = 8
CONV_W = 4
LRU_C = 8.0
N_HEADS = 4
HEAD_DIM = 64
V_DIM = 2 * HEAD_DIM
D_QK = N_HEADS * 2 * HEAD_DIM
D_ATT = N_HEADS * V_DIM
SEG = 512
ROPE_THETA = 10000.0
EPS = 1e-6
SCALE = HEAD_DIM ** -0.5
NEG_INF = -1e30
LAMBDA_INIT = 0.8 - 0.6 * math.exp(-0.3 * 0)

SUBLANES = 8
LANES = 128
V7X_VMEM_BYTES = 64 * 1024 * 1024
VMEM_LIMIT = 56 * 1024 * 1024

PROMPT_TILE = 512
ATTN_TQ = 512
ATTN_TK = 512
META_PAD = 128


def _dot(a, b):
    return jnp.dot(a, b, preferred_element_type=F32)


def _dot_nt(a, b):
    return lax.dot_general(a, b, (((1,), (1,)), ((), ())), preferred_element_type=F32)


def _sigmoid(x):
    return jax.nn.sigmoid(x)


def _softplus(z):
    return jnp.maximum(z, 0.0) + jnp.log1p(jnp.exp(-jnp.abs(z)))


def _group_rms(x, w, gmat_ref):
    x2 = x * x
    hi = x2.astype(BF16)
    lo = (x2 - hi.astype(F32)).astype(BF16)
    g = gmat_ref[...]
    ms = (_dot(hi, g) + _dot(lo, g)) * (1.0 / HEAD_DIM)
    return x * lax.rsqrt(ms + EPS) * w


def _rope(x, cos, sin_signed):
    lane = lax.broadcasted_iota(jnp.int32, x.shape, 1)
    first_half = (lane & (HEAD_DIM - 1)) < (HEAD_DIM // 2)
    width = x.shape[1]
    swapped = jnp.where(first_half,
                        pltpu.roll(x, width - HEAD_DIM // 2, 1),
                        pltpu.roll(x, HEAD_DIM // 2, 1))
    return x * cos + swapped * sin_signed


def _front(x, normw, win_ref, qnw, knw, gmat_ref, cos128, sin128):
    xn = x * lax.rsqrt(jnp.mean(x * x, axis=-1, keepdims=True) + EPS) * normw
    xb = xn.astype(BF16)
    xa, ga, uq, uk, v, gb = [_dot(xb, win_ref[:, i * SEG:(i + 1) * SEG]) for i in range(6)]
    reps = SEG // cos128.shape[1]
    cos = jnp.concatenate([cos128] * reps, axis=1)
    sin = jnp.concatenate([sin128] * reps, axis=1)
    q = _rope(_group_rms(uq, qnw, gmat_ref), cos, sin)
    k = _rope(_group_rms(uk, knw, gmat_ref), cos, sin)
    return xa, ga, q, k, v, gb


def _lru_coeffs(cf, wg_ref, bg, sp):
    g = _dot(cf.astype(BF16), wg_ref[...]) + bg
    r = _sigmoid(g[:, :D_LRU])
    ig = _sigmoid(g[:, D_LRU:])
    log_a = (-LRU_C) * r * sp
    a = jnp.exp(log_a)
    b = jnp.sqrt(-jnp.tanh(log_a) * (a * a + 1.0)) * (ig * cf)
    return a, b


def _scan8(a8, b8, h):
    row = lax.broadcasted_iota(jnp.int32, a8.shape, 0)
    for d in (1, 2, 4):
        keep = row >= d
        a_sh = jnp.where(keep, pltpu.roll(a8, d, 0), 1.0)
        b_sh = jnp.where(keep, pltpu.roll(b8, d, 0), 0.0)
        b8 = a8 * b_sh + b8
        a8 = a8 * a_sh
    return a8 * h + b8


def _conv(xs_ref, convw_ref, convb, n):
    out = convb
    for j in range(CONV_W):
        out = out + xs_ref[pl.ds(SUBLANES - (CONV_W - 1) + j, n), :] * convw_ref[j:j + 1, :]
    return out


def _prompt_front_kernel(x_ref, meta_ref, normw_ref, win_ref, convw_ref, convb_ref, wg_ref, bg_ref,
                         lam_ref, qnw_ref, knw_ref, gmat_ref, cos_ref, sin_ref, cosm_ref, sinm_ref,
                         ya_ref, q_ref, kf_ref, vf_ref, kb_ref, vt_ref, gb_ref,
                         kmf_ref, vmf_ref, kmb_ref, vmt_ref, hlast_ref, ctail_ref,
                         xs_ref, a_ref, b_ref, h_ref, hcar_ref):
    i = pl.program_id(0)
    sp = _softplus(-lam_ref[...])

    def lru(xa, n):
        xs_ref[pl.ds(SUBLANES, n), :] = xa
        cf = _conv(xs_ref, convw_ref, convb_ref[...], n)
        a, b = _lru_coeffs(cf, wg_ref, bg_ref[...], sp)
        a_ref[pl.ds(0, n), :] = a
        b_ref[pl.ds(0, n), :] = b
        xs_ref[pl.ds(0, SUBLANES), :] = xs_ref[pl.ds(n, SUBLANES), :]

        def body(g, h):
            r0 = pl.multiple_of(g * SUBLANES, SUBLANES)
            hh = _scan8(a_ref[pl.ds(r0, SUBLANES), :], b_ref[pl.ds(r0, SUBLANES), :], h)
            h_ref[pl.ds(r0, SUBLANES), :] = hh
            return jnp.broadcast_to(hh[SUBLANES - 1:SUBLANES, :], hh.shape)

        hcar_ref[...] = lax.fori_loop(0, n // SUBLANES, body, hcar_ref[...], unroll=4)

    @pl.when(i == 0)
    def _():
        xs_ref[pl.ds(0, SUBLANES), :] = jnp.zeros((SUBLANES, D_LRU), F32)
        hcar_ref[...] = jnp.zeros_like(hcar_ref)
        xa, _, _, k, v, _ = _front(meta_ref[...], normw_ref[...], win_ref, qnw_ref[...], knw_ref[...],
                                   gmat_ref, cosm_ref[...], sinm_ref[...])
        lru(xa, N_META)
        kmf_ref[...] = k
        vmf_ref[...] = v
        kmb_ref[pl.ds(0, N_META), :] = k.astype(BF16)
        kmb_ref[pl.ds(N_META, META_PAD - N_META), :] = jnp.zeros((META_PAD - N_META, D_QK), BF16)
        vpad = jnp.concatenate([v, jnp.zeros((META_PAD - N_META, D_ATT), F32)], axis=0)
        vmt_ref[...] = vpad.T.astype(BF16)

    n = x_ref.shape[0]
    xa, ga, q, k, v, gb = _front(x_ref[...], normw_ref[...], win_ref, qnw_ref[...], knw_ref[...],
                                 gmat_ref, cos_ref[...], sin_ref[...])
    lru(xa, n)
    ya_ref[...] = (h_ref[...] * (ga * _sigmoid(ga))).astype(BF16)
    q_ref[...] = (q * SCALE).astype(BF16)
    kf_ref[...] = k
    kb_ref[...] = k.astype(BF16)
    vf_ref[...] = v
    vt_ref[0] = v.T.astype(BF16)
    gb_ref[...] = gb * _sigmoid(gb)
    hlast_ref[...] = hcar_ref[0:1, :]
    ctail_ref[...] = xs_ref[pl.ds(SUBLANES - (CONV_W - 1), CONV_W - 1), :]


def _prompt_front(x, meta, normw, win_b, convw, convb, wg_b, bg, lam, qnw, knw, gmat, cos, sin, cosm, sinm):
    s = x.shape[0]
    tm = PROMPT_TILE
    nt = s // tm
    full = lambda shape: pl.BlockSpec(shape, lambda i: (0,) * len(shape))
    rows = lambda w: pl.BlockSpec((tm, w), lambda i: (i, 0))
    out_shape = (
        jax.ShapeDtypeStruct((s, D_LRU), BF16),
        jax.ShapeDtypeStruct((s, D_QK), BF16),
        jax.ShapeDtypeStruct((s, D_QK), F32),
        jax.ShapeDtypeStruct((s, D_ATT), F32),
        jax.ShapeDtypeStruct((s, D_QK), BF16),
        jax.ShapeDtypeStruct((nt, D_ATT, tm), BF16),
        jax.ShapeDtypeStruct((s, D_ATT), F32),
        jax.ShapeDtypeStruct((N_META, D_QK), F32),
        jax.ShapeDtypeStruct((N_META, D_ATT), F32),
        jax.ShapeDtypeStruct((META_PAD, D_QK), BF16),
        jax.ShapeDtypeStruct((D_ATT, META_PAD), BF16),
        jax.ShapeDtypeStruct((1, D_LRU), F32),
        jax.ShapeDtypeStruct((CONV_W - 1, D_LRU), F32),
    )
    out_specs = (
        rows(D_LRU), rows(D_QK), rows(D_QK), rows(D_ATT), rows(D_QK),
        pl.BlockSpec((1, D_ATT, tm), lambda i: (i, 0, 0)),
        rows(D_ATT),
        full((N_META, D_QK)), full((N_META, D_ATT)), full((META_PAD, D_QK)), full((D_ATT, META_PAD)),
        full((1, D_LRU)), full((CONV_W - 1, D_LRU)),
    )
    in_specs = [
        rows(D_MODEL), full(meta.shape), full(normw.shape), full(win_b.shape), full(convw.shape),
        full(convb.shape), full(wg_b.shape), full(bg.shape), full(lam.shape), full(qnw.shape),
        full(knw.shape), full(gmat.shape), rows(LANES), rows(LANES), full(cosm.shape), full(sinm.shape),
    ]
    return pl.pallas_call(
        _prompt_front_kernel,
        grid=(nt,),
        in_specs=in_specs,
        out_specs=out_specs,
        out_shape=out_shape,
        scratch_shapes=[
            pltpu.VMEM((SUBLANES + tm, D_LRU), F32),
            pltpu.VMEM((tm, D_LRU), F32),
            pltpu.VMEM((tm, D_LRU), F32),
            pltpu.VMEM((tm, D_LRU), F32),
            pltpu.VMEM((SUBLANES, D_LRU), F32),
        ],
        compiler_params=pltpu.CompilerParams(dimension_semantics=("arbitrary",),
                                             vmem_limit_bytes=VMEM_LIMIT),
        name="prompt_front",
    )(x, meta, normw, win_b, convw, convb, wg_b, bg, lam, qnw, knw, gmat, cos, sin, cosm, sinm)


def _lambda_full(lq1, lk1, lq2, lk2):
    s1 = jnp.sum(lq1 * lk1, axis=-1, keepdims=True)
    s2 = jnp.sum(lq2 * lk2, axis=-1, keepdims=True)
    return jnp.exp(s1) - jnp.exp(s2) + LAMBDA_INIT


def _prompt_attn_kernel(q_ref, k_ref, vt_ref, km_ref, vmt_ref, gb_ref, lq1_ref, lk1_ref, lq2_ref, lk2_ref,
                        sw_ref, o_ref, m_ref, l_ref, acc_ref):
    qb = pl.program_id(1)
    tq = q_ref.shape[0]
    tk = ATTN_TK
    q = q_ref[...]
    lane = lax.broadcasted_iota(jnp.int32, q.shape, 1)
    zero = jnp.zeros_like(q)
    qq = jnp.concatenate([jnp.where(lane < HEAD_DIM, q, zero), jnp.where(lane >= HEAD_DIM, q, zero)], axis=0)

    m_ref[...] = jnp.full_like(m_ref, NEG_INF)
    l_ref[...] = jnp.zeros_like(l_ref)
    acc_ref[...] = jnp.zeros_like(acc_ref)

    def step(kblk, vtblk, mask):
        s = _dot_nt(kblk, qq)
        if mask is not None:
            s = jnp.where(mask, s, NEG_INF)
        m_old = m_ref[...]
        m_new = jnp.maximum(m_old, jnp.max(s, axis=0, keepdims=True))
        alpha = jnp.exp(m_old - m_new)
        p = jnp.exp(s - m_new)
        l_ref[...] = alpha * l_ref[...] + jnp.sum(p, axis=0, keepdims=True)
        m_ref[...] = m_new
        pb = p.astype(BF16)
        for c in range(2):
            sl = slice(c * tq, (c + 1) * tq)
            acc_ref[c] = acc_ref[c] * alpha[:, sl] + _dot(vtblk, pb[:, sl])

    key_m = lax.broadcasted_iota(jnp.int32, (META_PAD, 2 * tq), 0)
    step(km_ref[...], vmt_ref[...], key_m < N_META)

    def body(j, carry):
        r0 = pl.multiple_of(j * tk, tk)
        step(k_ref[pl.ds(r0, tk), :], vt_ref[j], None)
        return carry

    lax.fori_loop(0, qb, body, 0)

    key_c = lax.broadcasted_iota(jnp.int32, (tk, 2 * tq), 0) >> CHUNK_SHIFT
    qry_c = (lax.broadcasted_iota(jnp.int32, (tk, 2 * tq), 1) & (tq - 1)) >> CHUNK_SHIFT
    r0 = pl.multiple_of(qb * tk, tk)
    step(k_ref[pl.ds(r0, tk), :], vt_ref[qb], key_c <= qry_c)

    lam = _lambda_full(lq1_ref[...], lk1_ref[...], lq2_ref[...], lk2_ref[...])
    inv_l = 1.0 / l_ref[...]
    o = acc_ref[0] * inv_l[:, :tq] - lam * (acc_ref[1] * inv_l[:, tq:])
    on = o * lax.rsqrt(jnp.mean(o * o, axis=0, keepdims=True) + EPS)
    y = on.T * sw_ref[...] * (1.0 - LAMBDA_INIT) * gb_ref[...]
    o_ref[...] = y.astype(BF16)


def _prompt_attn(q, kb, vt, kmb, vmt, sgb, lq1, lk1, lq2, lk2, sw):
    s = q.shape[0]
    tq = ATTN_TQ
    assert ATTN_TK == ATTN_TQ == vt.shape[2]
    nq = s // tq
    small = lambda a: pl.BlockSpec(a.shape, lambda h, i: (0, 0))
    return pl.pallas_call(
        _prompt_attn_kernel,
        grid=(N_HEADS, nq),
        in_specs=[
            pl.BlockSpec((tq, V_DIM), lambda h, i: (i, h)),
            pl.BlockSpec((s, V_DIM), lambda h, i: (0, h)),
            pl.BlockSpec((vt.shape[0], V_DIM, vt.shape[2]), lambda h, i: (0, h, 0)),
            pl.BlockSpec((META_PAD, V_DIM), lambda h, i: (0, h)),
            pl.BlockSpec((V_DIM, META_PAD), lambda h, i: (h, 0)),
            pl.BlockSpec((tq, V_DIM), lambda h, i: (i, h)),
            small(lq1), small(lk1), small(lq2), small(lk2), small(sw),
        ],
        out_specs=pl.BlockSpec((tq, V_DIM), lambda h, i: (i, h)),
        out_shape=jax.ShapeDtypeStruct((s, D_ATT), BF16),
        scratch_shapes=[
            pltpu.VMEM((1, 2 * tq), F32),
            pltpu.VMEM((1, 2 * tq), F32),
            pltpu.VMEM((2, V_DIM, tq), F32),
        ],
        compiler_params=pltpu.CompilerParams(dimension_semantics=("arbitrary", "arbitrary"),
                                             vmem_limit_bytes=VMEM_LIMIT),
        name="prompt_attn",
    )(q, kb, vt, kmb, vmt, sgb, lq1, lk1, lq2, lk2, sw)


def _out_proj_kernel(x_ref, ya_ref, yb_ref, wo_ref, y_ref):
    mix = _dot(ya_ref[...], wo_ref[0:D_LRU, :]) + _dot(yb_ref[...], wo_ref[D_LRU:D_LRU + D_ATT, :])
    y_ref[...] = x_ref[...] + mix


def _out_proj(x, ya, yb, wo_b, tm, name):
    s = x.shape[0]
    return pl.pallas_call(
        _out_proj_kernel,
        grid=(s // tm,),
        in_specs=[
            pl.BlockSpec((tm, D_MODEL), lambda i: (i, 0)),
            pl.BlockSpec((tm, D_LRU), lambda i: (i, 0)),
            pl.BlockSpec((tm, D_ATT), lambda i: (i, 0)),
            pl.BlockSpec(wo_b.shape, lambda i: (0, 0)),
        ],
        out_specs=pl.BlockSpec((tm, D_MODEL), lambda i: (i, 0)),
        out_shape=jax.ShapeDtypeStruct((s, D_MODEL), F32),
        compiler_params=pltpu.CompilerParams(dimension_semantics=("arbitrary",)),
        name=name,
    )(x, ya, yb, wo_b)


def _sample_front_kernel(x_ref, h0_ref, cprev_ref, normw_ref, win_ref, convw_ref, convb_ref, wg_ref, bg_ref,
                         lam_ref, qnw_ref, knw_ref, gmat_ref, cos_ref, sin_ref,
                         ya_ref, q_ref, k_ref, v_ref, gb_ref, hlast_ref, ctail_ref,
                         xs_ref, cf_ref, a_ref, b_ref, h_ref):
    nb = h0_ref.shape[0]
    steps = x_ref.shape[0] // nb
    sp = _softplus(-lam_ref[...])
    xa, ga, q, k, v, gb = _front(x_ref[...], normw_ref[...], win_ref, qnw_ref[...], knw_ref[...],
                                 gmat_ref, cos_ref[...], sin_ref[...])
    tail = CONV_W - 1
    for s in range(nb):
        xs_ref[pl.ds(SUBLANES - tail, tail), :] = cprev_ref[s]
        xs_ref[pl.ds(SUBLANES, steps), :] = xa[s * steps:(s + 1) * steps, :]
        cf_ref[pl.ds(s * steps, steps), :] = _conv(xs_ref, convw_ref, convb_ref[...], steps)
        ctail_ref[s] = xs_ref[pl.ds(SUBLANES + steps - tail, tail), :]
    cf = cf_ref[...]
    a, b = _lru_coeffs(cf, wg_ref, bg_ref[...], sp)
    a_ref[...] = a
    b_ref[...] = b
    for s in range(nb):
        h = jnp.broadcast_to(h0_ref[s:s + 1, :], (SUBLANES, D_LRU))
        for g in range(steps // SUBLANES):
            r0 = s * steps + g * SUBLANES
            hh = _scan8(a_ref[pl.ds(r0, SUBLANES), :], b_ref[pl.ds(r0, SUBLANES), :], h)
            h_ref[pl.ds(r0, SUBLANES), :] = hh
            h = jnp.broadcast_to(hh[SUBLANES - 1:SUBLANES, :], hh.shape)
        hlast_ref[s:s + 1, :] = h[0:1, :]
    ya_ref[...] = (h_ref[...] * (ga * _sigmoid(ga))).astype(BF16)
    q_ref[...] = (q * SCALE).astype(BF16)
    k_ref[...] = k
    v_ref[...] = v
    gb_ref[...] = gb * _sigmoid(gb)


def _sample_front(x, h0, cprev, normw, win_b, convw, convb, wg_b, bg, lam, qnw, knw, gmat, cos, sin):
    n = x.shape[0]
    nb = h0.shape[0]
    steps = n // nb
    args = (x, h0, cprev, normw, win_b, convw, convb, wg_b, bg, lam, qnw, knw, gmat, cos, sin)
    full = lambda shape: pl.BlockSpec(shape, lambda i: (0,) * len(shape))
    out_shape = (
        jax.ShapeDtypeStruct((n, D_LRU), BF16),
        jax.ShapeDtypeStruct((n, D_QK), BF16),
        jax.ShapeDtypeStruct((n, D_QK), F32),
        jax.ShapeDtypeStruct((n, D_ATT), F32),
        jax.ShapeDtypeStruct((n, D_ATT), F32),
        jax.ShapeDtypeStruct((nb, D_LRU), F32),
        jax.ShapeDtypeStruct((nb, CONV_W - 1, D_LRU), F32),
    )
    return pl.pallas_call(
        _sample_front_kernel,
        grid=(1,),
        in_specs=[full(a.shape) for a in args],
        out_specs=tuple(full(o.shape) for o in out_shape),
        out_shape=out_shape,
        scratch_shapes=[
            pltpu.VMEM((SUBLANES + steps, D_LRU), F32),
            pltpu.VMEM((n, D_LRU), F32),
            pltpu.VMEM((n, D_LRU), F32),
            pltpu.VMEM((n, D_LRU), F32),
            pltpu.VMEM((n, D_LRU), F32),
        ],
        compiler_params=pltpu.CompilerParams(dimension_semantics=("arbitrary",),
                                             vmem_limit_bytes=VMEM_LIMIT),
        name="sample_front",
    )(*args)


def _sample_attn_kernel(q_ref, kn_ref, vn_ref, kc_ref, vc_ref, gb_ref, lq1_ref, lk1_ref, lq2_ref, lk2_ref,
                        sw_ref, o_ref):
    q = q_ref[...]
    steps = q.shape[0]
    lane = lax.broadcasted_iota(jnp.int32, q.shape, 1)
    zero = jnp.zeros_like(q)
    ngroups = N_HEADS * 2
    group = lane >> HEAD_SHIFT
    qbd = jnp.concatenate([jnp.where(group == g, q, zero) for g in range(ngroups)], axis=0)
    kc = kc_ref[0].astype(BF16)
    kn = kn_ref[...].astype(BF16)
    s_c = _dot_nt(qbd, kc)
    s_n = _dot_nt(qbd, kn)
    m = jnp.maximum(jnp.max(s_c, axis=-1, keepdims=True), jnp.max(s_n, axis=-1, keepdims=True))
    p_c = jnp.exp(s_c - m)
    p_n = jnp.exp(s_n - m)
    l = jnp.sum(p_c, axis=-1, keepdims=True) + jnp.sum(p_n, axis=-1, keepdims=True)
    o = _dot(p_c.astype(BF16), vc_ref[0].astype(BF16)) + _dot(p_n.astype(BF16), vn_ref[...].astype(BF16))
    o = o * (1.0 / l)
    lam = _lambda_full(lq1_ref[...], lk1_ref[...], lq2_ref[...], lk2_ref[...])
    sgb = gb_ref[...]
    for h in range(N_HEADS):
        cols = slice(h * V_DIM, (h + 1) * V_DIM)
        o0 = o[(2 * h) * steps:(2 * h + 1) * steps, cols]
        o1 = o[(2 * h + 1) * steps:(2 * h + 2) * steps, cols]
        oh = o0 - lam * o1
        on = oh * lax.rsqrt(jnp.mean(oh * oh, axis=-1, keepdims=True) + EPS)
        o_ref[:, cols] = (on * sw_ref[...] * (1.0 - LAMBDA_INIT) * sgb[:, cols]).astype(BF16)


def _sample_attn(q, kn, vn, kc, vc, sgb, lq1, lk1, lq2, lk2, sw):
    nb, past, _ = kc.shape
    n = q.shape[0]
    steps = n // nb
    small = lambda a: pl.BlockSpec(a.shape, lambda b: (0, 0))
    rows = lambda w: pl.BlockSpec((steps, w), lambda b: (b, 0))
    return pl.pallas_call(
        _sample_attn_kernel,
        grid=(nb,),
        in_specs=[
            rows(D_QK), rows(D_QK), rows(D_ATT),
            pl.BlockSpec((1, past, D_QK), lambda b: (b, 0, 0)),
            pl.BlockSpec((1, past, D_ATT), lambda b: (b, 0, 0)),
            rows(D_ATT),
            small(lq1), small(lk1), small(lq2), small(lk2), small(sw),
        ],
        out_specs=rows(D_ATT),
        out_shape=jax.ShapeDtypeStruct((n, D_ATT), BF16),
        compiler_params=pltpu.CompilerParams(dimension_semantics=("arbitrary",),
                                             vmem_limit_bytes=VMEM_LIMIT),
        name="sample_attn",
    )(q, kn, vn, kc, vc, sgb, lq1, lk1, lq2, lk2, sw)


def _block_diag(w):
    nblk, c, d = w.shape
    eye = jnp.eye(nblk, dtype=w.dtype)
    return jnp.einsum("ncd,nm->ncmd", w, eye).reshape(nblk * c, nblk * d)


def _rope_tables(pos):
    inv = ROPE_THETA ** (-jnp.arange(0, HEAD_DIM, 2, dtype=F32) / HEAD_DIM)
    ang = pos.astype(F32)[:, None] * inv[None, :]
    cos, sin = jnp.cos(ang), jnp.sin(ang)
    reps = LANES // HEAD_DIM
    return (jnp.tile(jnp.concatenate([cos, cos], axis=1), (1, reps)),
            jnp.tile(jnp.concatenate([-sin, sin], axis=1), (1, reps)))


def kernel(x_prompt, x_sample, cache_k, cache_v, state_h, state_conv, meta_tokens, norm_w, w_in, w_out,
           conv_w, conv_b, gate_a_w, gate_a_b, gate_x_w, gate_x_b, lru_lambda, q_norm_w, k_norm_w,
           lambda_q1, lambda_k1, lambda_q2, lambda_k2, subln_w):
    depth = norm_w.shape[0]
    assert depth == 1 and x_prompt.shape[0] == 1
    s = x_prompt.shape[1]
    nb, steps = x_sample.shape[0], x_sample.shape[1]
    past = cache_k.shape[2]

    win_b = w_in[0].astype(BF16)
    wo_b = w_out[0].astype(BF16)
    wg_b = jnp.concatenate([_block_diag(gate_a_w[0]), _block_diag(gate_x_w[0])], axis=1).astype(BF16)
    bg = jnp.concatenate([gate_a_b[0].reshape(1, -1), gate_x_b[0].reshape(1, -1)], axis=1)
    qnw = jnp.tile(q_norm_w[0].reshape(1, -1), (1, D_QK // HEAD_DIM))
    knw = jnp.tile(k_norm_w[0].reshape(1, -1), (1, D_QK // HEAD_DIM))
    grp = jnp.arange(D_QK) // HEAD_DIM
    gmat = (grp[:, None] == grp[None, :]).astype(BF16)
    normw = norm_w[0].reshape(1, -1)
    convb = conv_b[0].reshape(1, -1)
    lam = lru_lambda[0].reshape(1, -1)
    sw = subln_w[0].reshape(1, -1)
    lq1, lk1 = lambda_q1[0].reshape(1, -1), lambda_k1[0].reshape(1, -1)
    lq2, lk2 = lambda_q2[0].reshape(1, -1), lambda_k2[0].reshape(1, -1)

    cos_p, sin_p = _rope_tables(jnp.arange(N_META + s, dtype=jnp.int32))
    cos_s, sin_s = _rope_tables(jnp.tile(past + jnp.arange(steps, dtype=jnp.int32), nb))

    xp = x_prompt[0]
    (ya, q, kf, vf, kb, vt, sgb, kmf, vmf, kmb, vmt, hlast, ctail) = _prompt_front(
        xp, meta_tokens, normw, win_b, conv_w[0], convb, wg_b, bg, lam, qnw, knw, gmat,
        cos_p[N_META:], sin_p[N_META:], cos_p[:N_META], sin_p[:N_META])
    yb = _prompt_attn(q, kb, vt, kmb, vmt, sgb, lq1, lk1, lq2, lk2, sw)
    y_prompt = _out_proj(xp, ya, yb, wo_b, PROMPT_TILE, "prompt_out_proj")

    xs = x_sample.reshape(nb * steps, D_MODEL)
    (ya_s, q_s, k_s, v_s, sgb_s, h_s, ctail_s) = _sample_front(
        xs, state_h[0], state_conv[0], normw, win_b, conv_w[0], convb, wg_b, bg, lam, qnw, knw, gmat,
        cos_s, sin_s)
    yb_s = _sample_attn(q_s, k_s, v_s, cache_k[0].reshape(nb, past, D_QK), cache_v[0].reshape(nb, past, D_ATT),
                        sgb_s, lq1, lk1, lq2, lk2, sw)
    y_sample = _out_proj(xs, ya_s, yb_s, wo_b, nb * steps, "sample_out_proj")

    k_prompt = jnp.concatenate([kmf, kf], axis=0).reshape(1, 1, N_META + s, N_HEADS, 2, HEAD_DIM)
    v_prompt = jnp.concatenate([vmf, vf], axis=0).reshape(1, 1, N_META + s, N_HEADS, V_DIM)
    return (
        y_prompt.reshape(1, s, D_MODEL),
        y_sample.reshape(nb, steps, D_MODEL),
        k_prompt,
        v_prompt,
        hlast.reshape(1, 1, D_LRU),
        ctail.reshape(1, 1, CONV_W - 1, D_LRU),
        k_s.reshape(1, nb, steps, N_HEADS, 2, HEAD_DIM),
        v_s.reshape(1, nb, steps, N_HEADS, V_DIM),
        h_s.reshape(1, nb, D_LRU),
        ctail_s.reshape(1, nb, CONV_W - 1, D_LRU),
    )
```

```python
import functools
import math

import jax
import jax.numpy as jnp
from jax import lax
from jax.experimental import pallas as pl
from jax.experimental.pallas import tpu as pltpu

F32 = jnp.float32
BF16 = jnp.bfloat16

D_MODEL = 1024
N_META = 16
CHUNK = 64
CHUNK_SHIFT = 6
HEAD_SHIFT = 6
D_LRU = 512
LRU_BLOCKS = 8
CONV_W = 4
LRU_C = 8.0
N_HEADS = 4
HEAD_DIM = 64
V_DIM = 2 * HEAD_DIM
D_QK = N_HEADS * 2 * HEAD_DIM
D_ATT = N_HEADS * V_DIM
SEG = 512
ROPE_THETA = 10000.0
EPS = 1e-6
SCALE = HEAD_DIM ** -0.5
LOG2E = math.log2(math.e)
Q_SCALE = SCALE * LOG2E
NEG_INF = -1e30
ONES_ROWS = 16
VT_ROWS = V_DIM + ONES_ROWS
LAMBDA_INIT = 0.8 - 0.6 * math.exp(-0.3 * 0)

SUBLANES = 8
LANES = 128
V7X_VMEM_BYTES = 64 * 1024 * 1024
VMEM_LIMIT = 56 * 1024 * 1024

PROMPT_TILE = 512
ATTN_TQ = 512
ATTN_TK = 512
META_PAD = 128


def _dot(a, b):
    return jnp.dot(a, b, preferred_element_type=F32)


def _dot_nt(a, b):
    return lax.dot_general(a, b, (((1,), (1,)), ((), ())), preferred_element_type=F32)


def _sigmoid(x):
    return jax.nn.sigmoid(x)


def _softplus(z):
    return jnp.maximum(z, 0.0) + jnp.log1p(jnp.exp(-jnp.abs(z)))


def _group_rms(x, w, gmat_ref):
    x2 = x * x
    hi = x2.astype(BF16)
    lo = (x2 - hi.astype(F32)).astype(BF16)
    g = gmat_ref[...]
    ms = (_dot(hi, g) + _dot(lo, g)) * (1.0 / HEAD_DIM)
    return x * lax.rsqrt(ms + EPS) * w


def _rope(x, cos, sin_signed):
    lane = lax.broadcasted_iota(jnp.int32, x.shape, 1)
    first_half = (lane & (HEAD_DIM - 1)) < (HEAD_DIM // 2)
    width = x.shape[1]
    swapped = jnp.where(first_half,
                        pltpu.roll(x, width - HEAD_DIM // 2, 1),
                        pltpu.roll(x, HEAD_DIM // 2, 1))
    return x * cos + swapped * sin_signed


def _front(x, normw, win_ref, qnw, knw, gmat_ref, cos128, sin128):
    xn = x * lax.rsqrt(jnp.mean(x * x, axis=-1, keepdims=True) + EPS) * normw
    xb = xn.astype(BF16)
    xa, ga, uq, uk, v, gb = [_dot(xb, win_ref[:, i * SEG:(i + 1) * SEG]) for i in range(6)]
    reps = SEG // cos128.shape[1]
    cos = jnp.concatenate([cos128] * reps, axis=1)
    sin = jnp.concatenate([sin128] * reps, axis=1)
    q = _rope(_group_rms(uq, qnw, gmat_ref), cos, sin)
    k = _rope(_group_rms(uk, knw, gmat_ref), cos, sin)
    return xa, ga, q, k, v, gb


def _lru_coeffs(cf, wg_ref, bg, sp):
    g = _dot(cf.astype(BF16), wg_ref[...]) + bg
    r = _sigmoid(g[:, :D_LRU])
    ig = _sigmoid(g[:, D_LRU:])
    log_a = (-LRU_C) * r * sp
    a = jnp.exp(log_a)
    b = jnp.sqrt(-jnp.tanh(log_a) * (a * a + 1.0)) * (ig * cf)
    return a, b


def _scan8(a8, b8, h):
    row = lax.broadcasted_iota(jnp.int32, a8.shape, 0)
    for d in (1, 2, 4):
        keep = row >= d
        a_sh = jnp.where(keep, pltpu.roll(a8, d, 0), 1.0)
        b_sh = jnp.where(keep, pltpu.roll(b8, d, 0), 0.0)
        b8 = a8 * b_sh + b8
        a8 = a8 * a_sh
    return a8 * h + b8


def _conv(xs_ref, convw_ref, convb, n):
    out = convb
    for j in range(CONV_W):
        out = out + xs_ref[pl.ds(SUBLANES - (CONV_W - 1) + j, n), :] * convw_ref[j:j + 1, :]
    return out


def _store_vt(vt_ref, vt):
    n = vt.shape[1]
    for h in range(N_HEADS):
        vt_ref[h, pl.ds(0, V_DIM), :] = vt[h * V_DIM:(h + 1) * V_DIM, :]
        vt_ref[h, pl.ds(V_DIM, ONES_ROWS), :] = jnp.ones((ONES_ROWS, n), BF16)


def _prompt_front_kernel(x_ref, meta_ref, normw_ref, win_ref, convw_ref, convb_ref, wg_ref, bg_ref,
                         lam_ref, qnw_ref, knw_ref, gmat_ref, cos_ref, sin_ref, cosm_ref, sinm_ref,
                         ya_ref, q_ref, kf_ref, vf_ref, kb_ref, vt_ref, gb_ref,
                         kmf_ref, vmf_ref, kmb_ref, vmt_ref, hlast_ref, ctail_ref,
                         xs_ref, a_ref, b_ref, h_ref, hcar_ref):
    i = pl.program_id(0)
    sp = _softplus(-lam_ref[...])

    def lru(xa, n):
        xs_ref[pl.ds(SUBLANES, n), :] = xa
        cf = _conv(xs_ref, convw_ref, convb_ref[...], n)
        a, b = _lru_coeffs(cf, wg_ref, bg_ref[...], sp)
        a_ref[pl.ds(0, n), :] = a
        b_ref[pl.ds(0, n), :] = b
        xs_ref[pl.ds(0, SUBLANES), :] = xs_ref[pl.ds(n, SUBLANES), :]

        def body(g, h):
            r0 = pl.multiple_of(g * SUBLANES, SUBLANES)
            hh = _scan8(a_ref[pl.ds(r0, SUBLANES), :], b_ref[pl.ds(r0, SUBLANES), :], h)
            h_ref[pl.ds(r0, SUBLANES), :] = hh
            return jnp.broadcast_to(hh[SUBLANES - 1:SUBLANES, :], hh.shape)

        hcar_ref[...] = lax.fori_loop(0, n // SUBLANES, body, hcar_ref[...], unroll=4)

    @pl.when(i == 0)
    def _():
        xs_ref[pl.ds(0, SUBLANES), :] = jnp.zeros((SUBLANES, D_LRU), F32)
        hcar_ref[...] = jnp.zeros_like(hcar_ref)
        xa, _, _, k, v, _ = _front(meta_ref[...], normw_ref[...], win_ref, qnw_ref[...], knw_ref[...],
                                   gmat_ref, cosm_ref[...], sinm_ref[...])
        lru(xa, N_META)
        kmf_ref[...] = k
        vmf_ref[...] = v
        kmb_ref[pl.ds(0, N_META), :] = k.astype(BF16)
        kmb_ref[pl.ds(N_META, META_PAD - N_META), :] = jnp.zeros((META_PAD - N_META, D_QK), BF16)
        vpad = jnp.concatenate([v, jnp.zeros((META_PAD - N_META, D_ATT), F32)], axis=0)
        _store_vt(vmt_ref, vpad.T.astype(BF16))

    n = x_ref.shape[0]
    xa, ga, q, k, v, gb = _front(x_ref[...], normw_ref[...], win_ref, qnw_ref[...], knw_ref[...],
                                 gmat_ref, cos_ref[...], sin_ref[...])
    lru(xa, n)
    ya_ref[...] = (h_ref[...] * (ga * _sigmoid(ga))).astype(BF16)
    q_ref[...] = (q * Q_SCALE).astype(BF16)
    kf_ref[...] = k
    kb_ref[...] = k.astype(BF16)
    vf_ref[...] = v
    _store_vt(vt_ref.at[0], v.T.astype(BF16))
    gb_ref[...] = gb * _sigmoid(gb)
    hlast_ref[...] = hcar_ref[0:1, :]
    ctail_ref[...] = xs_ref[pl.ds(SUBLANES - (CONV_W - 1), CONV_W - 1), :]


def _prompt_front(x, meta, normw, win_b, convw, convb, wg_b, bg, lam, qnw, knw, gmat, cos, sin, cosm, sinm):
    s = x.shape[0]
    tm = PROMPT_TILE
    nt = s // tm
    full = lambda shape: pl.BlockSpec(shape, lambda i: (0,) * len(shape))
    rows = lambda w: pl.BlockSpec((tm, w), lambda i: (i, 0))
    out_shape = (
        jax.ShapeDtypeStruct((s, D_LRU), BF16),
        jax.ShapeDtypeStruct((s, D_QK), BF16),
        jax.ShapeDtypeStruct((s, D_QK), F32),
        jax.ShapeDtypeStruct((s, D_ATT), F32),
        jax.ShapeDtypeStruct((s, D_QK), BF16),
        jax.ShapeDtypeStruct((nt, N_HEADS, VT_ROWS, tm), BF16),
        jax.ShapeDtypeStruct((s, D_ATT), F32),
        jax.ShapeDtypeStruct((N_META, D_QK), F32),
        jax.ShapeDtypeStruct((N_META, D_ATT), F32),
        jax.ShapeDtypeStruct((META_PAD, D_QK), BF16),
        jax.ShapeDtypeStruct((N_HEADS, VT_ROWS, META_PAD), BF16),
        jax.ShapeDtypeStruct((1, D_LRU), F32),
        jax.ShapeDtypeStruct((CONV_W - 1, D_LRU), F32),
    )
    out_specs = (
        rows(D_LRU), rows(D_QK), rows(D_QK), rows(D_ATT), rows(D_QK),
        pl.BlockSpec((1, N_HEADS, VT_ROWS, tm), lambda i: (i, 0, 0, 0)),
        rows(D_ATT),
        full((N_META, D_QK)), full((N_META, D_ATT)), full((META_PAD, D_QK)),
        full((N_HEADS, VT_ROWS, META_PAD)),
        full((1, D_LRU)), full((CONV_W - 1, D_LRU)),
    )
    in_specs = [
        rows(D_MODEL), full(meta.shape), full(normw.shape), full(win_b.shape), full(convw.shape),
        full(convb.shape), full(wg_b.shape), full(bg.shape), full(lam.shape), full(qnw.shape),
        full(knw.shape), full(gmat.shape), rows(LANES), rows(LANES), full(cosm.shape), full(sinm.shape),
    ]
    return pl.pallas_call(
        _prompt_front_kernel,
        grid=(nt,),
        in_specs=in_specs,
        out_specs=out_specs,
        out_shape=out_shape,
        scratch_shapes=[
            pltpu.VMEM((SUBLANES + tm, D_LRU), F32),
            pltpu.VMEM((tm, D_LRU), F32),
            pltpu.VMEM((tm, D_LRU), F32),
            pltpu.VMEM((tm, D_LRU), F32),
            pltpu.VMEM((SUBLANES, D_LRU), F32),
        ],
        compiler_params=pltpu.CompilerParams(dimension_semantics=("arbitrary",),
                                             vmem_limit_bytes=VMEM_LIMIT),
        name="prompt_front",
    )(x, meta, normw, win_b, convw, convb, wg_b, bg, lam, qnw, knw, gmat, cos, sin, cosm, sinm)


def _lambda_full(lq1, lk1, lq2, lk2):
    s1 = jnp.sum(lq1 * lk1, axis=-1, keepdims=True)
    s2 = jnp.sum(lq2 * lk2, axis=-1, keepdims=True)
    return jnp.exp(s1) - jnp.exp(s2) + LAMBDA_INIT


def _prompt_attn_kernel(q_ref, k_ref, vt_ref, km_ref, vmt_ref, gb_ref, lq1_ref, lk1_ref, lq2_ref, lk2_ref,
                        sw_ref, o_ref,
                        m_ref, acc_ref, pm_ref, am_ref,
                        s0_ref, s1_ref, mb0_ref, mb1_ref, p0_ref, p1_ref, a0_ref, a1_ref):
    qb = pl.program_id(1)
    tq = q_ref.shape[0]
    tk = ATTN_TK
    q = q_ref[...]
    lane = lax.broadcasted_iota(jnp.int32, q.shape, 1)
    zero = jnp.zeros_like(q)
    qq = jnp.concatenate([jnp.where(lane < HEAD_DIM, q, zero), jnp.where(lane >= HEAD_DIM, q, zero)], axis=0)

    m_ref[...] = jnp.full_like(m_ref, NEG_INF)
    acc_ref[...] = jnp.zeros_like(acc_ref)

    def scores(blk):
        r0 = pl.multiple_of(blk * tk, tk)
        return _dot_nt(k_ref[pl.ds(r0, tk), :], qq)

    def scores_to(blk, s_ref, mb_ref):
        s = scores(blk)
        s_ref[...] = s
        mb_ref[...] = jnp.max(s, axis=0, keepdims=True)

    def softmax(s, mb, p_ref, a_ref):
        m_old = m_ref[...]
        m_new = jnp.maximum(m_old, mb)
        a_ref[...] = jnp.exp2(m_old - m_new)
        m_ref[...] = m_new
        p_ref[...] = jnp.exp2(s - m_new).astype(BF16)

    def accumulate(p_ref, a_ref, vt):
        alpha = a_ref[...]
        for c in range(2):
            sl = slice(c * tq, (c + 1) * tq)
            acc_ref[c] = acc_ref[c] * alpha[:, sl] + _dot(vt, p_ref[:, sl])

    s_m = _dot_nt(km_ref[...], qq)
    s_m = jnp.where(lax.broadcasted_iota(jnp.int32, s_m.shape, 0) < N_META, s_m, NEG_INF)
    softmax(s_m, jnp.max(s_m, axis=0, keepdims=True), pm_ref, am_ref)
    accumulate(pm_ref, am_ref, vmt_ref[...])

    key_c = lax.broadcasted_iota(jnp.int32, (tk, 2 * tq), 0) >> CHUNK_SHIFT
    qry_c = (lax.broadcasted_iota(jnp.int32, (tk, 2 * tq), 1) & (tq - 1)) >> CHUNK_SHIFT
    s_d = jnp.where(key_c <= qry_c, scores(qb), NEG_INF)
    softmax(s_d, jnp.max(s_d, axis=0, keepdims=True), p0_ref, a0_ref)
    scores_to(0, s1_ref, mb1_ref)

    def pair(i, carry):
        t1 = 2 * i + 1
        scores_to(t1, s0_ref, mb0_ref)
        softmax(s1_ref[...], mb1_ref[...], p1_ref, a1_ref)
        accumulate(p0_ref, a0_ref, vt_ref[jnp.where(i == 0, qb, t1 - 2)])
        t2 = t1 + 1
        scores_to(t2, s1_ref, mb1_ref)
        softmax(s0_ref[...], mb0_ref[...], p0_ref, a0_ref)
        accumulate(p1_ref, a1_ref, vt_ref[t2 - 2])
        return carry

    lax.fori_loop(0, qb // 2, pair, 0)

    @pl.when(qb % 2 == 0)
    def _():
        accumulate(p0_ref, a0_ref, vt_ref[jnp.maximum(qb - 1, 0)])

    @pl.when(qb % 2 == 1)
    def _():
        softmax(s1_ref[...], mb1_ref[...], p1_ref, a1_ref)
        accumulate(p0_ref, a0_ref, vt_ref[jnp.where(qb == 1, qb, qb - 2)])
        accumulate(p1_ref, a1_ref, vt_ref[qb - 1])

    lam = _lambda_full(lq1_ref[...], lk1_ref[...], lq2_ref[...], lk2_ref[...])
    acc0 = acc_ref[0]
    acc1 = acc_ref[1]
    o = (acc0[:V_DIM, :] * (1.0 / acc0[V_DIM:V_DIM + 1, :])
         - lam * (acc1[:V_DIM, :] * (1.0 / acc1[V_DIM:V_DIM + 1, :])))
    on = o * lax.rsqrt(jnp.mean(o * o, axis=0, keepdims=True) + EPS)
    y = on.T * sw_ref[...] * (1.0 - LAMBDA_INIT) * gb_ref[...]
    o_ref[...] = y.astype(BF16)


def _prompt_attn(q, kb, vt, kmb, vmt, sgb, lq1, lk1, lq2, lk2, sw):
    s = q.shape[0]
    tq = ATTN_TQ
    tk = ATTN_TK
    assert tk == tq == vt.shape[3]
    nq = s // tq
    small = lambda a: pl.BlockSpec(a.shape, lambda h, i: (0, 0))
    return pl.pallas_call(
        _prompt_attn_kernel,
        grid=(N_HEADS, nq),
        in_specs=[
            pl.BlockSpec((tq, V_DIM), lambda h, i: (i, h)),
            pl.BlockSpec((s, V_DIM), lambda h, i: (0, h)),
            pl.BlockSpec((vt.shape[0], None, VT_ROWS, tk), lambda h, i: (0, h, 0, 0)),
            pl.BlockSpec((META_PAD, V_DIM), lambda h, i: (0, h)),
            pl.BlockSpec((None, VT_ROWS, META_PAD), lambda h, i: (h, 0, 0)),
            pl.BlockSpec((tq, V_DIM), lambda h, i: (i, h)),
            small(lq1), small(lk1), small(lq2), small(lk2), small(sw),
        ],
        out_specs=pl.BlockSpec((tq, V_DIM), lambda h, i: (i, h)),
        out_shape=jax.ShapeDtypeStruct((s, D_ATT), BF16),
        scratch_shapes=[
            pltpu.VMEM((1, 2 * tq), F32),
            pltpu.VMEM((2, VT_ROWS, tq), F32),
            pltpu.VMEM((META_PAD, 2 * tq), BF16),
            pltpu.VMEM((1, 2 * tq), F32),
            pltpu.VMEM((tk, 2 * tq), F32),
            pltpu.VMEM((tk, 2 * tq), F32),
            pltpu.VMEM((1, 2 * tq), F32),
            pltpu.VMEM((1, 2 * tq), F32),
            pltpu.VMEM((tk, 2 * tq), BF16),
            pltpu.VMEM((tk, 2 * tq), BF16),
            pltpu.VMEM((1, 2 * tq), F32),
            pltpu.VMEM((1, 2 * tq), F32),
        ],
        compiler_params=pltpu.CompilerParams(dimension_semantics=("arbitrary", "arbitrary"),
                                             vmem_limit_bytes=VMEM_LIMIT),
        name="prompt_attn",
    )(q, kb, vt, kmb, vmt, sgb, lq1, lk1, lq2, lk2, sw)


def _out_proj_kernel(x_ref, ya_ref, yb_ref, wo_ref, y_ref):
    mix = _dot(ya_ref[...], wo_ref[0:D_LRU, :]) + _dot(yb_ref[...], wo_ref[D_LRU:D_LRU + D_ATT, :])
    y_ref[...] = x_ref[...] + mix


def _out_proj(x, ya, yb, wo_b, tm, name):
    s = x.shape[0]
    return pl.pallas_call(
        _out_proj_kernel,
        grid=(s // tm,),
        in_specs=[
            pl.BlockSpec((tm, D_MODEL), lambda i: (i, 0)),
            pl.BlockSpec((tm, D_LRU), lambda i: (i, 0)),
            pl.BlockSpec((tm, D_ATT), lambda i: (i, 0)),
            pl.BlockSpec(wo_b.shape, lambda i: (0, 0)),
        ],
        out_specs=pl.BlockSpec((tm, D_MODEL), lambda i: (i, 0)),
        out_shape=jax.ShapeDtypeStruct((s, D_MODEL), F32),
        compiler_params=pltpu.CompilerParams(dimension_semantics=("arbitrary",)),
        name=name,
    )(x, ya, yb, wo_b)


def _sample_front_kernel(x_ref, h0_ref, cprev_ref, normw_ref, win_ref, convw_ref, convb_ref, wg_ref, bg_ref,
                         lam_ref, qnw_ref, knw_ref, gmat_ref, cos_ref, sin_ref,
                         ya_ref, q_ref, k_ref, v_ref, gb_ref, hlast_ref, ctail_ref,
                         xs_ref, cf_ref, a_ref, b_ref, h_ref):
    nb = h0_ref.shape[0]
    steps = x_ref.shape[0] // nb
    sp = _softplus(-lam_ref[...])
    xa, ga, q, k, v, gb = _front(x_ref[...], normw_ref[...], win_ref, qnw_ref[...], knw_ref[...],
                                 gmat_ref, cos_ref[...], sin_ref[...])
    tail = CONV_W - 1
    for s in range(nb):
        xs_ref[pl.ds(SUBLANES - tail, tail), :] = cprev_ref[s]
        xs_ref[pl.ds(SUBLANES, steps), :] = xa[s * steps:(s + 1) * steps, :]
        cf_ref[pl.ds(s * steps, steps), :] = _conv(xs_ref, convw_ref, convb_ref[...], steps)
        ctail_ref[s] = xs_ref[pl.ds(SUBLANES + steps - tail, tail), :]
    cf = cf_ref[...]
    a, b = _lru_coeffs(cf, wg_ref, bg_ref[...], sp)
    a_ref[...] = a
    b_ref[...] = b
    for s in range(nb):
        h = jnp.broadcast_to(h0_ref[s:s + 1, :], (SUBLANES, D_LRU))
        for g in range(steps // SUBLANES):
            r0 = s * steps + g * SUBLANES
            hh = _scan8(a_ref[pl.ds(r0, SUBLANES), :], b_ref[pl.ds(r0, SUBLANES), :], h)
            h_ref[pl.ds(r0, SUBLANES), :] = hh
            h = jnp.broadcast_to(hh[SUBLANES - 1:SUBLANES, :], hh.shape)
        hlast_ref[s:s + 1, :] = h[0:1, :]
    ya_ref[...] = (h_ref[...] * (ga * _sigmoid(ga))).astype(BF16)
    q_ref[...] = (q * Q_SCALE).astype(BF16)
    k_ref[...] = k
    v_ref[...] = v
    gb_ref[...] = gb * _sigmoid(gb)


def _sample_front(x, h0, cprev, normw, win_b, convw, convb, wg_b, bg, lam, qnw, knw, gmat, cos, sin):
    n = x.shape[0]
    nb = h0.shape[0]
    steps = n // nb
    args = (x, h0, cprev, normw, win_b, convw, convb, wg_b, bg, lam, qnw, knw, gmat, cos, sin)
    full = lambda shape: pl.BlockSpec(shape, lambda i: (0,) * len(shape))
    out_shape = (
        jax.ShapeDtypeStruct((n, D_LRU), BF16),
        jax.ShapeDtypeStruct((n, D_QK), BF16),
        jax.ShapeDtypeStruct((n, D_QK), F32),
        jax.ShapeDtypeStruct((n, D_ATT), F32),
        jax.ShapeDtypeStruct((n, D_ATT), F32),
        jax.ShapeDtypeStruct((nb, D_LRU), F32),
        jax.ShapeDtypeStruct((nb, CONV_W - 1, D_LRU), F32),
    )
    return pl.pallas_call(
        _sample_front_kernel,
        grid=(1,),
        in_specs=[full(a.shape) for a in args],
        out_specs=tuple(full(o.shape) for o in out_shape),
        out_shape=out_shape,
        scratch_shapes=[
            pltpu.VMEM((SUBLANES + steps, D_LRU), F32),
            pltpu.VMEM((n, D_LRU), F32),
            pltpu.VMEM((n, D_LRU), F32),
            pltpu.VMEM((n, D_LRU), F32),
            pltpu.VMEM((n, D_LRU), F32),
        ],
        compiler_params=pltpu.CompilerParams(dimension_semantics=("arbitrary",),
                                             vmem_limit_bytes=VMEM_LIMIT),
        name="sample_front",
    )(*args)


def _sample_attn_kernel(q_ref, kn_ref, vn_ref, kc_ref, vc_ref, gb_ref, lq1_ref, lk1_ref, lq2_ref, lk2_ref,
                        sw_ref, o_ref):
    q = q_ref[...]
    steps = q.shape[0]
    lane = lax.broadcasted_iota(jnp.int32, q.shape, 1)
    zero = jnp.zeros_like(q)
    ngroups = N_HEADS * 2
    group = lane >> HEAD_SHIFT
    qbd = jnp.concatenate([jnp.where(group == g, q, zero) for g in range(ngroups)], axis=0)
    kc = kc_ref[0].astype(BF16)
    kn = kn_ref[...].astype(BF16)
    s_c = _dot_nt(qbd, kc)
    s_n = _dot_nt(qbd, kn)
    m = jnp.maximum(jnp.max(s_c, axis=-1, keepdims=True), jnp.max(s_n, axis=-1, keepdims=True))
    p_c = jnp.exp2(s_c - m)
    p_n = jnp.exp2(s_n - m)
    l = jnp.sum(p_c, axis=-1, keepdims=True) + jnp.sum(p_n, axis=-1, keepdims=True)
    o = _dot(p_c.astype(BF16), vc_ref[0].astype(BF16)) + _dot(p_n.astype(BF16), vn_ref[...].astype(BF16))
    o = o * (1.0 / l)
    lam = _lambda_full(lq1_ref[...], lk1_ref[...], lq2_ref[...], lk2_ref[...])
    sgb = gb_ref[...]
    for h in range(N_HEADS):
        cols = slice(h * V_DIM, (h + 1) * V_DIM)
        o0 = o[(2 * h) * steps:(2 * h + 1) * steps, cols]
        o1 = o[(2 * h + 1) * steps:(2 * h + 2) * steps, cols]
        oh = o0 - lam * o1
        on = oh * lax.rsqrt(jnp.mean(oh * oh, axis=-1, keepdims=True) + EPS)
        o_ref[:, cols] = (on * sw_ref[...] * (1.0 - LAMBDA_INIT) * sgb[:, cols]).astype(BF16)


def _sample_attn(q, kn, vn, kc, vc, sgb, lq1, lk1, lq2, lk2, sw):
    nb, past, _ = kc.shape
    n = q.shape[0]
    steps = n // nb
    small = lambda a: pl.BlockSpec(a.shape, lambda b: (0, 0))
    rows = lambda w: pl.BlockSpec((steps, w), lambda b: (b, 0))
    return pl.pallas_call(
        _sample_attn_kernel,
        grid=(nb,),
        in_specs=[
            rows(D_QK), rows(D_QK), rows(D_ATT),
            pl.BlockSpec((1, past, D_QK), lambda b: (b, 0, 0)),
            pl.BlockSpec((1, past, D_ATT), lambda b: (b, 0, 0)),
            rows(D_ATT),
            small(lq1), small(lk1), small(lq2), small(lk2), small(sw),
        ],
        out_specs=rows(D_ATT),
        out_shape=jax.ShapeDtypeStruct((n, D_ATT), BF16),
        compiler_params=pltpu.CompilerParams(dimension_semantics=("arbitrary",),
                                             vmem_limit_bytes=VMEM_LIMIT),
        name="sample_attn",
    )(q, kn, vn, kc, vc, sgb, lq1, lk1, lq2, lk2, sw)


def _block_diag(w):
    nblk, c, d = w.shape
    eye = jnp.eye(nblk, dtype=w.dtype)
    return jnp.einsum("ncd,nm->ncmd", w, eye).reshape(nblk * c, nblk * d)


def _rope_tables(pos):
    inv = ROPE_THETA ** (-jnp.arange(0, HEAD_DIM, 2, dtype=F32) / HEAD_DIM)
    ang = pos.astype(F32)[:, None] * inv[None, :]
    cos, sin = jnp.cos(ang), jnp.sin(ang)
    reps = LANES // HEAD_DIM
    return (jnp.tile(jnp.concatenate([cos, cos], axis=1), (1, reps)),
            jnp.tile(jnp.concatenate([-sin, sin], axis=1), (1, reps)))


def kernel(x_prompt, x_sample, cache_k, cache_v, state_h, state_conv, meta_tokens, norm_w, w_in, w_out,
           conv_w, conv_b, gate_a_w, gate_a_b, gate_x_w, gate_x_b, lru_lambda, q_norm_w, k_norm_w,
           lambda_q1, lambda_k1, lambda_q2, lambda_k2, subln_w):
    depth = norm_w.shape[0]
    assert depth == 1 and x_prompt.shape[0] == 1
    s = x_prompt.shape[1]
    nb, steps = x_sample.shape[0], x_sample.shape[1]
    past = cache_k.shape[2]

    win_b = w_in[0].astype(BF16)
    wo_b = w_out[0].astype(BF16)
    wg_b = jnp.concatenate([_block_diag(gate_a_w[0]), _block_diag(gate_x_w[0])], axis=1).astype(BF16)
    bg = jnp.concatenate([gate_a_b[0].reshape(1, -1), gate_x_b[0].reshape(1, -1)], axis=1)
    qnw = jnp.tile(q_norm_w[0].reshape(1, -1), (1, D_QK // HEAD_DIM))
    knw = jnp.tile(k_norm_w[0].reshape(1, -1), (1, D_QK // HEAD_DIM))
    grp = jnp.arange(D_QK) // HEAD_DIM
    gmat = (grp[:, None] == grp[None, :]).astype(BF16)
    normw = norm_w[0].reshape(1, -1)
    convb = conv_b[0].reshape(1, -1)
    lam = lru_lambda[0].reshape(1, -1)
    sw = subln_w[0].reshape(1, -1)
    lq1, lk1 = lambda_q1[0].reshape(1, -1), lambda_k1[0].reshape(1, -1)
    lq2, lk2 = lambda_q2[0].reshape(1, -1), lambda_k2[0].reshape(1, -1)

    cos_p, sin_p = _rope_tables(jnp.arange(N_META + s, dtype=jnp.int32))
    cos_s, sin_s = _rope_tables(jnp.tile(past + jnp.arange(steps, dtype=jnp.int32), nb))

    xp = x_prompt[0]
    (ya, q, kf, vf, kb, vt, sgb, kmf, vmf, kmb, vmt, hlast, ctail) = _prompt_front(
        xp, meta_tokens, normw, win_b, conv_w[0], convb, wg_b, bg, lam, qnw, knw, gmat,
        cos_p[N_META:], sin_p[N_META:], cos_p[:N_META], sin_p[:N_META])
    yb = _prompt_attn(q, kb, vt, kmb, vmt, sgb, lq1, lk1, lq2, lk2, sw)
    y_prompt = _out_proj(xp, ya, yb, wo_b, PROMPT_TILE, "prompt_out_proj")

    xs = x_sample.reshape(nb * steps, D_MODEL)
    (ya_s, q_s, k_s, v_s, sgb_s, h_s, ctail_s) = _sample_front(
        xs, state_h[0], state_conv[0], normw, win_b, conv_w[0], convb, wg_b, bg, lam, qnw, knw, gmat,
        cos_s, sin_s)
    yb_s = _sample_attn(q_s, k_s, v_s, cache_k[0].reshape(nb, past, D_QK), cache_v[0].reshape(nb, past, D_ATT),
                        sgb_s, lq1, lk1, lq2, lk2, sw)
    y_sample = _out_proj(xs, ya_s, yb_s, wo_b, nb * steps, "sample_out_proj")

    k_prompt = jnp.concatenate([kmf, kf], axis=0).reshape(1, 1, N_META + s, N_HEADS, 2, HEAD_DIM)
    v_prompt = jnp.concatenate([vmf, vf], axis=0).reshape(1, 1, N_META + s, N_HEADS, V_DIM)
    return (
        y_prompt.reshape(1, s, D_MODEL),
        y_sample.reshape(nb, steps, D_MODEL),
        k_prompt,
        v_prompt,
        hlast.reshape(1, 1, D_LRU),
        ctail.reshape(1, 1, CONV_W - 1, D_LRU),
        k_s.reshape(1, nb, steps, N_HEADS, 2, HEAD_DIM),
        v_s.reshape(1, nb, steps, N_HEADS, V_DIM),
        h_s.reshape(1, nb, D_LRU),
        ctail_s.reshape(1, nb, CONV_W - 1, D_LRU),
    )
```

```python
import functools
import math

import jax
import jax.numpy as jnp
from jax import lax
from jax.experimental import pallas as pl
from jax.experimental.pallas import tpu as pltpu

F32 = jnp.float32
BF16 = jnp.bfloat16

D_MODEL = 1024
N_META = 16
CHUNK = 64
CHUNK_SHIFT = 6
HEAD_SHIFT = 6
D_LRU = 512
LRU_BLOCKS = 8
CONV_W = 4
LRU_C = 8.0
N_HEADS = 4
HEAD_DIM = 64
V_DIM = 2 * HEAD_DIM
D_QK = N_HEADS * 2 * HEAD_DIM
D_ATT = N_HEADS * V_DIM
SEG = 512
ROPE_THETA = 10000.0
EPS = 1e-6
SCALE = HEAD_DIM ** -0.5
LOG2E = math.log2(math.e)
Q_SCALE = SCALE * LOG2E
NEG_INF = -1e30
ONES_ROWS = 16
VT_ROWS = V_DIM + ONES_ROWS
LAMBDA_INIT = 0.8 - 0.6 * math.exp(-0.3 * 0)

SUBLANES = 8
LANES = 128
V7X_VMEM_BYTES = 64 * 1024 * 1024
VMEM_LIMIT = 56 * 1024 * 1024

PROMPT_TILE = 512
ATTN_TQ = 512
ATTN_TK = 512
META_PAD = 128


def _dot(a, b):
    return jnp.dot(a, b, preferred_element_type=F32)


def _dot_nt(a, b):
    return lax.dot_general(a, b, (((1,), (1,)), ((), ())), preferred_element_type=F32)


def _sigmoid(x):
    return jax.nn.sigmoid(x)


def _softplus(z):
    return jnp.maximum(z, 0.0) + jnp.log1p(jnp.exp(-jnp.abs(z)))


def _group_rms(x, w, gmat_ref):
    x2 = x * x
    hi = x2.astype(BF16)
    lo = (x2 - hi.astype(F32)).astype(BF16)
    g = gmat_ref[...]
    ms = (_dot(hi, g) + _dot(lo, g)) * (1.0 / HEAD_DIM)
    return x * lax.rsqrt(ms + EPS) * w


def _rope(x, cos, sin_signed):
    lane = lax.broadcasted_iota(jnp.int32, x.shape, 1)
    first_half = (lane & (HEAD_DIM - 1)) < (HEAD_DIM // 2)
    width = x.shape[1]
    swapped = jnp.where(first_half,
                        pltpu.roll(x, width - HEAD_DIM // 2, 1),
                        pltpu.roll(x, HEAD_DIM // 2, 1))
    return x * cos + swapped * sin_signed


def _front(x, normw, win_ref, qnw, knw, gmat_ref, cos128, sin128):
    xn = x * lax.rsqrt(jnp.mean(x * x, axis=-1, keepdims=True) + EPS) * normw
    xb = xn.astype(BF16)
    xa, ga, uq, uk, v, gb = [_dot(xb, win_ref[:, i * SEG:(i + 1) * SEG]) for i in range(6)]
    reps = SEG // cos128.shape[1]
    cos = jnp.concatenate([cos128] * reps, axis=1)
    sin = jnp.concatenate([sin128] * reps, axis=1)
    q = _rope(_group_rms(uq, qnw, gmat_ref), cos, sin)
    k = _rope(_group_rms(uk, knw, gmat_ref), cos, sin)
    return xa, ga, q, k, v, gb


def _lru_coeffs(cf, wg_ref, bg, sp):
    g = _dot(cf.astype(BF16), wg_ref[...]) + bg
    r = _sigmoid(g[:, :D_LRU])
    ig = _sigmoid(g[:, D_LRU:])
    log_a = (-LRU_C) * r * sp
    a = jnp.exp(log_a)
    b = jnp.sqrt(-jnp.tanh(log_a) * (a * a + 1.0)) * (ig * cf)
    return a, b


def _scan8(a8, b8, h):
    row = lax.broadcasted_iota(jnp.int32, a8.shape, 0)
    for d in (1, 2, 4):
        keep = row >= d
        a_sh = jnp.where(keep, pltpu.roll(a8, d, 0), 1.0)
        b_sh = jnp.where(keep, pltpu.roll(b8, d, 0), 0.0)
        b8 = a8 * b_sh + b8
        a8 = a8 * a_sh
    return a8 * h + b8


def _conv(xs_ref, convw_ref, convb, n):
    out = convb
    for j in range(CONV_W):
        out = out + xs_ref[pl.ds(SUBLANES - (CONV_W - 1) + j, n), :] * convw_ref[j:j + 1, :]
    return out


def _store_vt(vt_ref, vt):
    n = vt.shape[1]
    for h in range(N_HEADS):
        vt_ref[h, pl.ds(0, V_DIM), :] = vt[h * V_DIM:(h + 1) * V_DIM, :]
        vt_ref[h, pl.ds(V_DIM, ONES_ROWS), :] = jnp.ones((ONES_ROWS, n), BF16)


def _prompt_front_kernel(x_ref, meta_ref, normw_ref, win_ref, convw_ref, convb_ref, wg_ref, bg_ref,
                         lam_ref, qnw_ref, knw_ref, gmat_ref, cos_ref, sin_ref, cosm_ref, sinm_ref,
                         ya_ref, q_ref, kf_ref, vf_ref, kb_ref, vt_ref, gb_ref,
                         kmf_ref, vmf_ref, kmb_ref, vmt_ref, hlast_ref, ctail_ref,
                         xs_ref, a_ref, b_ref, h_ref, hcar_ref):
    i = pl.program_id(0)
    sp = _softplus(-lam_ref[...])

    def lru(xa, n):
        xs_ref[pl.ds(SUBLANES, n), :] = xa
        cf = _conv(xs_ref, convw_ref, convb_ref[...], n)
        a, b = _lru_coeffs(cf, wg_ref, bg_ref[...], sp)
        a_ref[pl.ds(0, n), :] = a
        b_ref[pl.ds(0, n), :] = b
        xs_ref[pl.ds(0, SUBLANES), :] = xs_ref[pl.ds(n, SUBLANES), :]

        def body(g, h):
            r0 = pl.multiple_of(g * SUBLANES, SUBLANES)
            hh = _scan8(a_ref[pl.ds(r0, SUBLANES), :], b_ref[pl.ds(r0, SUBLANES), :], h)
            h_ref[pl.ds(r0, SUBLANES), :] = hh
            return jnp.broadcast_to(hh[SUBLANES - 1:SUBLANES, :], hh.shape)

        hcar_ref[...] = lax.fori_loop(0, n // SUBLANES, body, hcar_ref[...], unroll=4)

    @pl.when(i == 0)
    def _():
        xs_ref[pl.ds(0, SUBLANES), :] = jnp.zeros((SUBLANES, D_LRU), F32)
        hcar_ref[...] = jnp.zeros_like(hcar_ref)
        xa, _, _, k, v, _ = _front(meta_ref[...], normw_ref[...], win_ref, qnw_ref[...], knw_ref[...],
                                   gmat_ref, cosm_ref[...], sinm_ref[...])
        lru(xa, N_META)
        kmf_ref[...] = k
        vmf_ref[...] = v
        kmb_ref[pl.ds(0, N_META), :] = k.astype(BF16)
        kmb_ref[pl.ds(N_META, META_PAD - N_META), :] = jnp.zeros((META_PAD - N_META, D_QK), BF16)
        vpad = jnp.concatenate([v, jnp.zeros((META_PAD - N_META, D_ATT), F32)], axis=0)
        _store_vt(vmt_ref, vpad.T.astype(BF16))

    n = x_ref.shape[0]
    xa, ga, q, k, v, gb = _front(x_ref[...], normw_ref[...], win_ref, qnw_ref[...], knw_ref[...],
                                 gmat_ref, cos_ref[...], sin_ref[...])
    lru(xa, n)
    ya_ref[...] = (h_ref[...] * (ga * _sigmoid(ga))).astype(BF16)
    q_ref[...] = (q * Q_SCALE).astype(BF16)
    kf_ref[...] = k
    kb_ref[...] = k.astype(BF16)
    vf_ref[...] = v
    _store_vt(vt_ref.at[0], v.T.astype(BF16))
    gb_ref[...] = gb * _sigmoid(gb)
    hlast_ref[...] = hcar_ref[0:1, :]
    ctail_ref[...] = xs_ref[pl.ds(SUBLANES - (CONV_W - 1), CONV_W - 1), :]


def _prompt_front(x, meta, normw, win_b, convw, convb, wg_b, bg, lam, qnw, knw, gmat, cos, sin, cosm, sinm):
    s = x.shape[0]
    tm = PROMPT_TILE
    nt = s // tm
    full = lambda shape: pl.BlockSpec(shape, lambda i: (0,) * len(shape))
    rows = lambda w: pl.BlockSpec((tm, w), lambda i: (i, 0))
    out_shape = (
        jax.ShapeDtypeStruct((s, D_LRU), BF16),
        jax.ShapeDtypeStruct((s, D_QK), BF16),
        jax.ShapeDtypeStruct((s, D_QK), F32),
        jax.ShapeDtypeStruct((s, D_ATT), F32),
        jax.ShapeDtypeStruct((s, D_QK), BF16),
        jax.ShapeDtypeStruct((nt, N_HEADS, VT_ROWS, tm), BF16),
        jax.ShapeDtypeStruct((s, D_ATT), F32),
        jax.ShapeDtypeStruct((N_META, D_QK), F32),
        jax.ShapeDtypeStruct((N_META, D_ATT), F32),
        jax.ShapeDtypeStruct((META_PAD, D_QK), BF16),
        jax.ShapeDtypeStruct((N_HEADS, VT_ROWS, META_PAD), BF16),
        jax.ShapeDtypeStruct((1, D_LRU), F32),
        jax.ShapeDtypeStruct((CONV_W - 1, D_LRU), F32),
    )
    out_specs = (
        rows(D_LRU), rows(D_QK), rows(D_QK), rows(D_ATT), rows(D_QK),
        pl.BlockSpec((1, N_HEADS, VT_ROWS, tm), lambda i: (i, 0, 0, 0)),
        rows(D_ATT),
        full((N_META, D_QK)), full((N_META, D_ATT)), full((META_PAD, D_QK)),
        full((N_HEADS, VT_ROWS, META_PAD)),
        full((1, D_LRU)), full((CONV_W - 1, D_LRU)),
    )
    in_specs = [
        rows(D_MODEL), full(meta.shape), full(normw.shape), full(win_b.shape), full(convw.shape),
        full(convb.shape), full(wg_b.shape), full(bg.shape), full(lam.shape), full(qnw.shape),
        full(knw.shape), full(gmat.shape), rows(LANES), rows(LANES), full(cosm.shape), full(sinm.shape),
    ]
    return pl.pallas_call(
        _prompt_front_kernel,
        grid=(nt,),
        in_specs=in_specs,
        out_specs=out_specs,
        out_shape=out_shape,
        scratch_shapes=[
            pltpu.VMEM((SUBLANES + tm, D_LRU), F32),
            pltpu.VMEM((tm, D_LRU), F32),
            pltpu.VMEM((tm, D_LRU), F32),
            pltpu.VMEM((tm, D_LRU), F32),
            pltpu.VMEM((SUBLANES, D_LRU), F32),
        ],
        compiler_params=pltpu.CompilerParams(dimension_semantics=("arbitrary",),
                                             vmem_limit_bytes=VMEM_LIMIT),
        name="prompt_front",
    )(x, meta, normw, win_b, convw, convb, wg_b, bg, lam, qnw, knw, gmat, cos, sin, cosm, sinm)


def _lambda_full(lq1, lk1, lq2, lk2):
    s1 = jnp.sum(lq1 * lk1, axis=-1, keepdims=True)
    s2 = jnp.sum(lq2 * lk2, axis=-1, keepdims=True)
    return jnp.exp(s1) - jnp.exp(s2) + LAMBDA_INIT


def _prompt_attn_kernel(q_ref, k_ref, vt_ref, km_ref, vmt_ref, gb_ref, lq1_ref, lk1_ref, lq2_ref, lk2_ref,
                        sw_ref, o_ref,
                        m_ref, acc_ref, pm_ref, am_ref,
                        s0_ref, s1_ref, mb0_ref, mb1_ref, p0_ref, p1_ref, a0_ref, a1_ref):
    qb = pl.program_id(1)
    tq = q_ref.shape[0]
    tk = ATTN_TK
    q = q_ref[...]
    lane = lax.broadcasted_iota(jnp.int32, q.shape, 1)
    zero = jnp.zeros_like(q)
    qq = jnp.concatenate([jnp.where(lane < HEAD_DIM, q, zero), jnp.where(lane >= HEAD_DIM, q, zero)], axis=0)

    m_ref[...] = jnp.full_like(m_ref, NEG_INF)
    acc_ref[...] = jnp.zeros_like(acc_ref)

    def scores(blk):
        r0 = pl.multiple_of(blk * tk, tk)
        return _dot_nt(k_ref[pl.ds(r0, tk), :], qq)

    def scores_to(blk, s_ref, mb_ref):
        s = scores(blk)
        s_ref[...] = s
        mb_ref[...] = jnp.max(s, axis=0, keepdims=True)

    def softmax(s, mb, p_ref, a_ref):
        m_old = m_ref[...]
        m_new = jnp.maximum(m_old, mb)
        a_ref[...] = jnp.exp2(m_old - m_new)
        m_ref[...] = m_new
        p_ref[...] = jnp.exp2(s - m_new).astype(BF16)

    def accumulate(p_ref, a_ref, vt):
        alpha = a_ref[...]
        for c in range(2):
            sl = slice(c * tq, (c + 1) * tq)
            acc_ref[c] = acc_ref[c] * alpha[:, sl] + _dot(vt, p_ref[:, sl])

    s_m = _dot_nt(km_ref[...], qq)
    s_m = jnp.where(lax.broadcasted_iota(jnp.int32, s_m.shape, 0) < N_META, s_m, NEG_INF)
    softmax(s_m, jnp.max(s_m, axis=0, keepdims=True), pm_ref, am_ref)
    accumulate(pm_ref, am_ref, vmt_ref[...])

    key_c = lax.broadcasted_iota(jnp.int32, (tk, 2 * tq), 0) >> CHUNK_SHIFT
    qry_c = (lax.broadcasted_iota(jnp.int32, (tk, 2 * tq), 1) & (tq - 1)) >> CHUNK_SHIFT
    s_d = jnp.where(key_c <= qry_c, scores(qb), NEG_INF)
    softmax(s_d, jnp.max(s_d, axis=0, keepdims=True), p0_ref, a0_ref)
    scores_to(0, s1_ref, mb1_ref)

    def pair(i, carry):
        t1 = 2 * i + 1
        scores_to(t1, s0_ref, mb0_ref)
        softmax(s1_ref[...], mb1_ref[...], p1_ref, a1_ref)
        accumulate(p0_ref, a0_ref, vt_ref[jnp.where(i == 0, qb, t1 - 2)])
        t2 = t1 + 1
        scores_to(t2, s1_ref, mb1_ref)
        softmax(s0_ref[...], mb0_ref[...], p0_ref, a0_ref)
        accumulate(p1_ref, a1_ref, vt_ref[t2 - 2])
        return carry

    lax.fori_loop(0, qb // 2, pair, 0)

    @pl.when(qb % 2 == 0)
    def _():
        accumulate(p0_ref, a0_ref, vt_ref[jnp.maximum(qb - 1, 0)])

    @pl.when(qb % 2 == 1)
    def _():
        softmax(s1_ref[...], mb1_ref[...], p1_ref, a1_ref)
        accumulate(p0_ref, a0_ref, vt_ref[jnp.where(qb == 1, qb, qb - 2)])
        accumulate(p1_ref, a1_ref, vt_ref[qb - 1])

    lam = _lambda_full(lq1_ref[...], lk1_ref[...], lq2_ref[...], lk2_ref[...])
    acc0 = acc_ref[0]
    acc1 = acc_ref[1]
    o = (acc0[:V_DIM, :] * (1.0 / acc0[V_DIM:V_DIM + 1, :])
         - lam * (acc1[:V_DIM, :] * (1.0 / acc1[V_DIM:V_DIM + 1, :])))
    on = o * lax.rsqrt(jnp.mean(o * o, axis=0, keepdims=True) + EPS)
    y = on.T * sw_ref[...] * (1.0 - LAMBDA_INIT) * gb_ref[...]
    o_ref[...] = y.astype(BF16)


def _prompt_attn(q, kb, vt, kmb, vmt, sgb, lq1, lk1, lq2, lk2, sw):
    s = q.shape[0]
    tq = ATTN_TQ
    tk = ATTN_TK
    assert tk == tq == vt.shape[3]
    nq = s // tq
    small = lambda a: pl.BlockSpec(a.shape, lambda h, i: (0, 0))
    return pl.pallas_call(
        _prompt_attn_kernel,
        grid=(N_HEADS, nq),
        in_specs=[
            pl.BlockSpec((tq, V_DIM), lambda h, i: (i, h)),
            pl.BlockSpec((s, V_DIM), lambda h, i: (0, h)),
            pl.BlockSpec((vt.shape[0], None, VT_ROWS, tk), lambda h, i: (0, h, 0, 0)),
            pl.BlockSpec((META_PAD, V_DIM), lambda h, i: (0, h)),
            pl.BlockSpec((None, VT_ROWS, META_PAD), lambda h, i: (h, 0, 0)),
            pl.BlockSpec((tq, V_DIM), lambda h, i: (i, h)),
            small(lq1), small(lk1), small(lq2), small(lk2), small(sw),
        ],
        out_specs=pl.BlockSpec((tq, V_DIM), lambda h, i: (i, h)),
        out_shape=jax.ShapeDtypeStruct((s, D_ATT), BF16),
        scratch_shapes=[
            pltpu.VMEM((1, 2 * tq), F32),
            pltpu.VMEM((2, VT_ROWS, tq), F32),
            pltpu.VMEM((META_PAD, 2 * tq), BF16),
            pltpu.VMEM((1, 2 * tq), F32),
            pltpu.VMEM((tk, 2 * tq), F32),
            pltpu.VMEM((tk, 2 * tq), F32),
            pltpu.VMEM((1, 2 * tq), F32),
            pltpu.VMEM((1, 2 * tq), F32),
            pltpu.VMEM((tk, 2 * tq), BF16),
            pltpu.VMEM((tk, 2 * tq), BF16),
            pltpu.VMEM((1, 2 * tq), F32),
            pltpu.VMEM((1, 2 * tq), F32),
        ],
        compiler_params=pltpu.CompilerParams(dimension_semantics=("arbitrary", "arbitrary"),
                                             vmem_limit_bytes=VMEM_LIMIT),
        name="prompt_attn",
    )(q, kb, vt, kmb, vmt, sgb, lq1, lk1, lq2, lk2, sw)


def _out_proj_kernel(x_ref, ya_ref, yb_ref, wo_ref, y_ref):
    mix = _dot(ya_ref[...], wo_ref[0:D_LRU, :]) + _dot(yb_ref[...], wo_ref[D_LRU:D_LRU + D_ATT, :])
    y_ref[...] = x_ref[...] + mix


def _out_proj(x, ya, yb, wo_b, tm, name):
    s = x.shape[0]
    return pl.pallas_call(
        _out_proj_kernel,
        grid=(s // tm,),
        in_specs=[
            pl.BlockSpec((tm, D_MODEL), lambda i: (i, 0)),
            pl.BlockSpec((tm, D_LRU), lambda i: (i, 0)),
            pl.BlockSpec((tm, D_ATT), lambda i: (i, 0)),
            pl.BlockSpec(wo_b.shape, lambda i: (0, 0)),
        ],
        out_specs=pl.BlockSpec((tm, D_MODEL), lambda i: (i, 0)),
        out_shape=jax.ShapeDtypeStruct((s, D_MODEL), F32),
        compiler_params=pltpu.CompilerParams(dimension_semantics=("arbitrary",)),
        name=name,
    )(x, ya, yb, wo_b)


def _sample_front_kernel(x_ref, h0_ref, cprev_ref, normw_ref, win_ref, convw_ref, convb_ref, wg_ref, bg_ref,
                         lam_ref, qnw_ref, knw_ref, gmat_ref, cos_ref, sin_ref,
                         ya_ref, q_ref, k_ref, v_ref, gb_ref, hlast_ref, ctail_ref,
                         xs_ref, cf_ref, a_ref, b_ref, h_ref):
    nb = h0_ref.shape[0]
    steps = x_ref.shape[0] // nb
    sp = _softplus(-lam_ref[...])
    xa, ga, q, k, v, gb = _front(x_ref[...], normw_ref[...], win_ref, qnw_ref[...], knw_ref[...],
                                 gmat_ref, cos_ref[...], sin_ref[...])
    tail = CONV_W - 1
    for s in range(nb):
        xs_ref[pl.ds(SUBLANES - tail, tail), :] = cprev_ref[s]
        xs_ref[pl.ds(SUBLANES, steps), :] = xa[s * steps:(s + 1) * steps, :]
        cf_ref[pl.ds(s * steps, steps), :] = _conv(xs_ref, convw_ref, convb_ref[...], steps)
        ctail_ref[s] = xs_ref[pl.ds(SUBLANES + steps - tail, tail), :]
    cf = cf_ref[...]
    a, b = _lru_coeffs(cf, wg_ref, bg_ref[...], sp)
    a_ref[...] = a
    b_ref[...] = b
    for s in range(nb):
        h = jnp.broadcast_to(h0_ref[s:s + 1, :], (SUBLANES, D_LRU))
        for g in range(steps // SUBLANES):
            r0 = s * steps + g * SUBLANES
            hh = _scan8(a_ref[pl.ds(r0, SUBLANES), :], b_ref[pl.ds(r0, SUBLANES), :], h)
            h_ref[pl.ds(r0, SUBLANES), :] = hh
            h = jnp.broadcast_to(hh[SUBLANES - 1:SUBLANES, :], hh.shape)
        hlast_ref[s:s + 1, :] = h[0:1, :]
    ya_ref[...] = (h_ref[...] * (ga * _sigmoid(ga))).astype(BF16)
    q_ref[...] = (q * Q_SCALE).astype(BF16)
    k_ref[...] = k
    v_ref[...] = v
    gb_ref[...] = gb * _sigmoid(gb)


def _sample_front(x, h0, cprev, normw, win_b, convw, convb, wg_b, bg, lam, qnw, knw, gmat, cos, sin):
    n = x.shape[0]
    nb = h0.shape[0]
    steps = n // nb
    args = (x, h0, cprev, normw, win_b, convw, convb, wg_b, bg, lam, qnw, knw, gmat, cos, sin)
    full = lambda shape: pl.BlockSpec(shape, lambda i: (0,) * len(shape))
    out_shape = (
        jax.ShapeDtypeStruct((n, D_LRU), BF16),
        jax.ShapeDtypeStruct((n, D_QK), BF16),
        jax.ShapeDtypeStruct((n, D_QK), F32),
        jax.ShapeDtypeStruct((n, D_ATT), F32),
        jax.ShapeDtypeStruct((n, D_ATT), F32),
        jax.ShapeDtypeStruct((nb, D_LRU), F32),
        jax.ShapeDtypeStruct((nb, CONV_W - 1, D_LRU), F32),
    )
    return pl.pallas_call(
        _sample_front_kernel,
        grid=(1,),
        in_specs=[full(a.shape) for a in args],
        out_specs=tuple(full(o.shape) for o in out_shape),
        out_shape=out_shape,
        scratch_shapes=[
            pltpu.VMEM((SUBLANES + steps, D_LRU), F32),
            pltpu.VMEM((n, D_LRU), F32),
            pltpu.VMEM((n, D_LRU), F32),
            pltpu.VMEM((n, D_LRU), F32),
            pltpu.VMEM((n, D_LRU), F32),
        ],
        compiler_params=pltpu.CompilerParams(dimension_semantics=("arbitrary",),
                                             vmem_limit_bytes=VMEM_LIMIT),
        name="sample_front",
    )(*args)


def _sample_attn_kernel(q_ref, kn_ref, vn_ref, kt_ref, vc_ref, gb_ref, lq1_ref, lk1_ref, lq2_ref, lk2_ref,
                        sw_ref, o_ref):
    q = q_ref[...]
    steps = q.shape[0]
    lane = lax.broadcasted_iota(jnp.int32, q.shape, 1)
    zero = jnp.zeros_like(q)
    ngroups = N_HEADS * 2
    group = lane >> HEAD_SHIFT
    qbd = jnp.concatenate([jnp.where(group == g, q, zero) for g in range(ngroups)], axis=0)
    kt = kt_ref[0].astype(BF16)
    past = kt.shape[1]
    kn = kn_ref[...].astype(BF16)
    s_c = _dot(qbd, kt)
    s_n = _dot_nt(qbd, kn)
    m = jnp.maximum(jnp.max(s_c, axis=-1, keepdims=True), jnp.max(s_n, axis=-1, keepdims=True))
    p_c = jnp.exp2(s_c - m)
    p_n = jnp.exp2(s_n - m)
    l = jnp.sum(p_c, axis=-1, keepdims=True) + jnp.sum(p_n, axis=-1, keepdims=True)
    inv_l = 1.0 / l
    p_c = p_c.astype(BF16)
    p_n = p_n.astype(BF16)
    vn = vn_ref[...].astype(BF16)
    lam = _lambda_full(lq1_ref[...], lk1_ref[...], lq2_ref[...], lk2_ref[...])
    sgb = gb_ref[...]
    for h in range(N_HEADS):
        cols = slice(h * V_DIM, (h + 1) * V_DIM)
        rows = slice(2 * h * steps, (2 * h + 2) * steps)
        v_h = vc_ref[0, pl.ds(h, past, stride=N_HEADS), :].astype(BF16)
        o_h = (_dot(p_c[rows, :], v_h) + _dot(p_n[rows, :], vn[:, cols])) * inv_l[rows, :]
        oh = o_h[:steps, :] - lam * o_h[steps:, :]
        on = oh * lax.rsqrt(jnp.mean(oh * oh, axis=-1, keepdims=True) + EPS)
        o_ref[:, cols] = (on * sw_ref[...] * (1.0 - LAMBDA_INIT) * sgb[:, cols]).astype(BF16)


def _sample_attn(q, kn, vn, kt, vc, sgb, lq1, lk1, lq2, lk2, sw):
    nb, _, past = kt.shape
    n = q.shape[0]
    steps = n // nb
    small = lambda a: pl.BlockSpec(a.shape, lambda b: (0, 0))
    rows = lambda w: pl.BlockSpec((steps, w), lambda b: (b, 0))
    return pl.pallas_call(
        _sample_attn_kernel,
        grid=(nb,),
        in_specs=[
            rows(D_QK), rows(D_QK), rows(D_ATT),
            pl.BlockSpec((1, D_QK, past), lambda b: (b, 0, 0)),
            pl.BlockSpec((1, past * N_HEADS, V_DIM), lambda b: (b, 0, 0)),
            rows(D_ATT),
            small(lq1), small(lk1), small(lq2), small(lk2), small(sw),
        ],
        out_specs=rows(D_ATT),
        out_shape=jax.ShapeDtypeStruct((n, D_ATT), BF16),
        compiler_params=pltpu.CompilerParams(dimension_semantics=("arbitrary",),
                                             vmem_limit_bytes=VMEM_LIMIT),
        name="sample_attn",
    )(q, kn, vn, kt, vc, sgb, lq1, lk1, lq2, lk2, sw)


def _block_diag(w):
    nblk, c, d = w.shape
    eye = jnp.eye(nblk, dtype=w.dtype)
    return jnp.einsum("ncd,nm->ncmd", w, eye).reshape(nblk * c, nblk * d)


def _rope_tables(pos):
    inv = ROPE_THETA ** (-jnp.arange(0, HEAD_DIM, 2, dtype=F32) / HEAD_DIM)
    ang = pos.astype(F32)[:, None] * inv[None, :]
    cos, sin = jnp.cos(ang), jnp.sin(ang)
    reps = LANES // HEAD_DIM
    return (jnp.tile(jnp.concatenate([cos, cos], axis=1), (1, reps)),
            jnp.tile(jnp.concatenate([-sin, sin], axis=1), (1, reps)))


def kernel(x_prompt, x_sample, cache_k, cache_v, state_h, state_conv, meta_tokens, norm_w, w_in, w_out,
           conv_w, conv_b, gate_a_w, gate_a_b, gate_x_w, gate_x_b, lru_lambda, q_norm_w, k_norm_w,
           lambda_q1, lambda_k1, lambda_q2, lambda_k2, subln_w):
    depth = norm_w.shape[0]
    assert depth == 1 and x_prompt.shape[0] == 1
    s = x_prompt.shape[1]
    nb, steps = x_sample.shape[0], x_sample.shape[1]
    past = cache_k.shape[2]

    win_b = w_in[0].astype(BF16)
    wo_b = w_out[0].astype(BF16)
    wg_b = jnp.concatenate([_block_diag(gate_a_w[0]), _block_diag(gate_x_w[0])], axis=1).astype(BF16)
    bg = jnp.concatenate([gate_a_b[0].reshape(1, -1), gate_x_b[0].reshape(1, -1)], axis=1)
    qnw = jnp.tile(q_norm_w[0].reshape(1, -1), (1, D_QK // HEAD_DIM))
    knw = jnp.tile(k_norm_w[0].reshape(1, -1), (1, D_QK // HEAD_DIM))
    grp = jnp.arange(D_QK) // HEAD_DIM
    gmat = (grp[:, None] == grp[None, :]).astype(BF16)
    normw = norm_w[0].reshape(1, -1)
    convb = conv_b[0].reshape(1, -1)
    lam = lru_lambda[0].reshape(1, -1)
    sw = subln_w[0].reshape(1, -1)
    lq1, lk1 = lambda_q1[0].reshape(1, -1), lambda_k1[0].reshape(1, -1)
    lq2, lk2 = lambda_q2[0].reshape(1, -1), lambda_k2[0].reshape(1, -1)

    cos_p, sin_p = _rope_tables(jnp.arange(N_META + s, dtype=jnp.int32))
    cos_s, sin_s = _rope_tables(jnp.tile(past + jnp.arange(steps, dtype=jnp.int32), nb))

    xp = x_prompt[0]
    (ya, q, kf, vf, kb, vt, sgb, kmf, vmf, kmb, vmt, hlast, ctail) = _prompt_front(
        xp, meta_tokens, normw, win_b, conv_w[0], convb, wg_b, bg, lam, qnw, knw, gmat,
        cos_p[N_META:], sin_p[N_META:], cos_p[:N_META], sin_p[:N_META])
    yb = _prompt_attn(q, kb, vt, kmb, vmt, sgb, lq1, lk1, lq2, lk2, sw)
    y_prompt = _out_proj(xp, ya, yb, wo_b, PROMPT_TILE, "prompt_out_proj")

    xs = x_sample.reshape(nb * steps, D_MODEL)
    (ya_s, q_s, k_s, v_s, sgb_s, h_s, ctail_s) = _sample_front(
        xs, state_h[0], state_conv[0], normw, win_b, conv_w[0], convb, wg_b, bg, lam, qnw, knw, gmat,
        cos_s, sin_s)
    kt_cache = jnp.transpose(cache_k[0], (0, 2, 3, 4, 1)).reshape(nb, D_QK, past)
    v_cache = cache_v[0].reshape(nb, past * N_HEADS, V_DIM)
    yb_s = _sample_attn(q_s, k_s, v_s, kt_cache, v_cache, sgb_s, lq1, lk1, lq2, lk2, sw)
    y_sample = _out_proj(xs, ya_s, yb_s, wo_b, nb * steps, "sample_out_proj")

    k_prompt = jnp.concatenate([kmf, kf], axis=0).reshape(1, 1, N_META + s, N_HEADS, 2, HEAD_DIM)
    v_prompt = jnp.concatenate([vmf, vf], axis=0).reshape(1, 1, N_META + s, N_HEADS, V_DIM)
    return (
        y_prompt.reshape(1, s, D_MODEL),
        y_sample.reshape(nb, steps, D_MODEL),
        k_prompt,
        v_prompt,
        hlast.reshape(1, 1, D_LRU),
        ctail.reshape(1, 1, CONV_W - 1, D_LRU),
        k_s.reshape(1, nb, steps, N_HEADS, 2, HEAD_DIM),
        v_s.reshape(1, nb, steps, N_HEADS, V_DIM),
        h_s.reshape(1, nb, D_LRU),
        ctail_s.reshape(1, nb, CONV_W - 1, D_LRU),
    )
```
